```python
import math
import jax, jax.numpy as jnp
from jax import lax
import numpy as np

D_MODEL = 1024
BATCH = 2
SEQ = 8192
DEPTH = 4

N_MIXERS = 2
D_FF = 2816
EPS = 1e-6
A_HEADS = 4
A_DK = 128
A_DV = 256
A_QK = A_HEADS * A_DK
A_V = A_HEADS * A_DV
A_CONV = 4
A_CHUNK = 64
A_PROJ = 2 * A_QK + 2 * A_V + 2 * A_HEADS
B_HEADS = 4
B_DH = 128
B_DV = 2 * B_DH
B_QK = B_HEADS * 2 * B_DH
B_V = B_HEADS * B_DV
B_PROJ = 2 * B_QK + B_V
ROPE_THETA = 500000.0
ROPE_DIM = B_DH // 4
Q_BLOCK = 128
N_A = (DEPTH + 1) // 2
N_B = DEPTH // 2

kernel_name = "hybrid_mlstm_diffattn_macaron_adaln"


def rmsnorm(x, g):
    x32 = x.astype(jnp.float32)
    y = x32 * lax.rsqrt(jnp.mean(x32 * x32, -1, keepdims=True) + EPS)
    return (y * g.astype(jnp.float32)).astype(x.dtype)


def modulate(x, g, shift, scale):
    return rmsnorm(x, g) * (1.0 + scale[:, None, :]) + shift[:, None, :]


def swiglu(h, w13, w2):
    gate, up = jnp.split(h @ w13, 2, axis=-1)
    return (jax.nn.silu(gate) * up) @ w2


def rope_partial(x, cos, sin):
    half = ROPE_DIM // 2
    c = cos[:, :, None, None, :]
    s = sin[:, :, None, None, :]
    x1 = x[..., :half]
    x2 = x[..., half:ROPE_DIM]
    return jnp.concatenate([x1 * c - x2 * s, x2 * c + x1 * s, x[..., ROPE_DIM:]], axis=-1)


def mlstm_cell(q, k, v, log_i, log_f):
    nb, H, S, dk = q.shape
    dv = v.shape[-1]
    L = A_CHUNK
    nc = S // L

    def to_chunks(t):
        return jnp.moveaxis(t.reshape(nb, H, nc, L, *t.shape[3:]), 2, 0)

    xs = tuple(to_chunks(t) for t in (q, k, v, log_i, log_f))
    causal = jnp.tril(jnp.ones((L, L), dtype=bool))

    def step(carry, inp):
        C, n, m = carry
        qc, kc, vc, ic, fc = inp
        b = jnp.cumsum(fc, axis=-1)
        a = b + m[..., None]
        dlog = jnp.where(causal, b[..., :, None] - b[..., None, :] + ic[..., None, :], -jnp.inf)
        m_t = jnp.maximum(a, dlog.max(-1))
        dw = jnp.exp(dlog - m_t[..., None])
        inter = jnp.exp(a - m_t)
        s = jnp.einsum('bhtd,bhsd->bhts', qc, kc) * dw
        num = inter[..., None] * jnp.einsum('bhtd,bhvd->bhtv', qc, C) + jnp.einsum('bhts,bhsv->bhtv', s, vc)
        den = inter * jnp.einsum('bhtd,bhd->bht', qc, n) + s.sum(-1)
        h = num / jnp.maximum(jnp.abs(den), jnp.exp(-m_t))[..., None]
        b_last = b[..., -1]
        wlog = b_last[..., None] - b + ic
        m_new = jnp.maximum(b_last + m, wlog.max(-1))
        decay = jnp.exp(b_last + m - m_new)
        w = jnp.exp(wlog - m_new[..., None])
        C_new = decay[..., None, None] * C + jnp.einsum('bhs,bhsv,bhsd->bhvd', w, vc, kc)
        n_new = decay[..., None] * n + jnp.einsum('bhs,bhsd->bhd', w, kc)
        return (C_new, n_new, m_new), h

    init = (jnp.zeros((nb, H, dv, dk), jnp.float32),
            jnp.zeros((nb, H, dk), jnp.float32),
            jnp.zeros((nb, H), jnp.float32))
    _, hs = lax.scan(step, init, xs)
    return jnp.moveaxis(hs, 0, 2).reshape(nb, H, S, dv)


def mlstm_mixer(h, w_in, conv_w, conv_b, b_if, norm_g, w_out):
    nb, S, _ = h.shape
    p = h @ w_in
    qk, v, o, gates = jnp.split(p, [2 * A_QK, 2 * A_QK + A_V, 2 * A_QK + 2 * A_V], axis=-1)
    qk = lax.conv_general_dilated(qk, conv_w[:, None, :], window_strides=(1,),
                                  padding=[(A_CONV - 1, 0)],
                                  dimension_numbers=('NWC', 'WIO', 'NWC'),
                                  feature_group_count=2 * A_QK) + conv_b
    qk = jax.nn.silu(qk)
    q, k = jnp.split(qk, 2, axis=-1)

    def heads(t, d):
        return t.reshape(nb, S, A_HEADS, d).transpose(0, 2, 1, 3).astype(jnp.float32)

    q = heads(q, A_DK)
    k = heads(k, A_DK) * (A_DK ** -0.5)
    v = heads(v, A_DV)
    gates = (gates + b_if).astype(jnp.float32).transpose(0, 2, 1)
    log_i = gates[:, :A_HEADS]
    log_f = jax.nn.log_sigmoid(gates[:, A_HEADS:])
    hh = mlstm_cell(q, k, v, log_i, log_f).transpose(0, 2, 1, 3)
    hh = hh * lax.rsqrt(jnp.mean(hh * hh, -1, keepdims=True) + EPS)
    hh = hh.reshape(nb, S, A_V) * norm_g.astype(jnp.float32)
    y = (jax.nn.sigmoid(o.astype(jnp.float32)) * hh).astype(h.dtype)
    return y @ w_out


def diff_attn_mixer(h, cos, sin, w_in, lam, norm_g, w_out, lam_init):
    nb, S, _ = h.shape
    p = h @ w_in
    q, k, v = jnp.split(p, [B_QK, 2 * B_QK], axis=-1)
    q = rope_partial(q.reshape(nb, S, B_HEADS, 2, B_DH).astype(jnp.float32), cos, sin) * (B_DH ** -0.5)
    k = rope_partial(k.reshape(nb, S, B_HEADS, 2, B_DH).astype(jnp.float32), cos, sin)
    v = v.reshape(nb, S, B_HEADS, B_DV).astype(jnp.float32).transpose(0, 2, 1, 3)
    lf = lam.astype(jnp.float32)
    lam_full = jnp.exp(jnp.sum(lf[0] * lf[1])) - jnp.exp(jnp.sum(lf[2] * lf[3])) + lam_init
    q = q.transpose(0, 2, 3, 1, 4)
    k = k.transpose(0, 2, 3, 1, 4)
    n_blk = S // Q_BLOCK
    qb = jnp.moveaxis(q.reshape(nb, B_HEADS, 2, n_blk, Q_BLOCK, B_DH), 3, 0)
    k_idx = jnp.arange(S)

    def block(args):
        qblk, bi = args
        q_idx = bi * Q_BLOCK + jnp.arange(Q_BLOCK)
        s = jnp.einsum('bhcqd,bhckd->bhcqk', qblk, k)
        s = jnp.where(k_idx[None, :] <= q_idx[:, None], s, -jnp.inf)
        pr = jax.nn.softmax(s, axis=-1)
        a = pr[:, :, 0] - lam_full * pr[:, :, 1]
        return jnp.einsum('bhqk,bhkd->bhqd', a, v)

    o = lax.map(block, (qb, jnp.arange(n_blk)))
    o = jnp.moveaxis(o, 0, 2).reshape(nb, B_HEADS, S, B_DV).transpose(0, 2, 1, 3)
    o = o * lax.rsqrt(jnp.mean(o * o, -1, keepdims=True) + EPS) * norm_g.astype(jnp.float32) * (1.0 - lam_init)
    return o.reshape(nb, S, B_V).astype(h.dtype) @ w_out


def setup_inputs(seed: int = 0) -> dict:
    key = jax.random.key(seed)
    ks = jax.random.split(key, 24)
    f32 = jnp.float32

    def nrm(k, shape, scale):
        return jax.random.normal(k, shape, f32) * scale

    x = nrm(ks[0], (BATCH, SEQ, D_MODEL), 1.0)
    c = nrm(ks[1], (BATCH, D_MODEL), 1.0)
    offs = jax.random.randint(ks[2], (BATCH, 1), 0, 1024, dtype=jnp.int32)
    positions = (offs + jnp.arange(SEQ, dtype=jnp.int32)[None, :]).astype(jnp.int32)

    ada_w = nrm(ks[3], (DEPTH, D_MODEL, 9 * D_MODEL), 0.3 * D_MODEL ** -0.5)
    ada_base = jnp.zeros((DEPTH, 3, 3, D_MODEL), f32).at[:, :, 2, :].set(1.0)
    ada_b = (ada_base + nrm(ks[4], (DEPTH, 3, 3, D_MODEL), 0.02)).reshape(DEPTH, 9 * D_MODEL)
    norm_g = 1.0 + nrm(ks[5], (DEPTH, 3, D_MODEL), 0.02)
    ffn_w13 = nrm(ks[6], (DEPTH, 2, D_MODEL, 2 * D_FF), D_MODEL ** -0.5)
    ffn_w2 = nrm(ks[7], (DEPTH, 2, D_FF, D_MODEL), D_FF ** -0.5)

    a_w_in = nrm(ks[8], (N_A, D_MODEL, A_PROJ), D_MODEL ** -0.5)
    a_conv_w = nrm(ks[9], (N_A, A_CONV, 2 * A_QK), A_CONV ** -0.5)
    a_conv_b = nrm(ks[10], (N_A, 2 * A_QK), 0.02)
    i_bias = nrm(ks[11], (N_A, A_HEADS), 0.1)
    f_bias = jnp.linspace(3.0, 6.0, A_HEADS, dtype=f32)[None, :] + nrm(ks[12], (N_A, A_HEADS), 0.1)
    a_b_if = jnp.concatenate([i_bias, f_bias], axis=-1)
    a_norm_g = 1.0 + nrm(ks[13], (N_A, A_V), 0.02)
    a_w_out = nrm(ks[14], (N_A, A_V, D_MODEL), A_V ** -0.5)

    b_w_in = nrm(ks[15], (N_B, D_MODEL, B_PROJ), D_MODEL ** -0.5)
    b_lam = nrm(ks[16], (N_B, 4, B_DH), 0.1)
    b_norm_g = 1.0 + nrm(ks[17], (N_B, B_DV), 0.02)
    b_w_out = nrm(ks[18], (N_B, B_V, D_MODEL), B_V ** -0.5)

    final_g = 1.0 + nrm(ks[19], (D_MODEL,), 0.02)
    return {"x": x, "c": c, "positions": positions, "ada_w": ada_w, "ada_b": ada_b,
            "norm_g": norm_g, "ffn_w13": ffn_w13, "ffn_w2": ffn_w2,
            "a_w_in": a_w_in, "a_conv_w": a_conv_w, "a_conv_b": a_conv_b, "a_b_if": a_b_if,
            "a_norm_g": a_norm_g, "a_w_out": a_w_out,
            "b_w_in": b_w_in, "b_lam": b_lam, "b_norm_g": b_norm_g, "b_w_out": b_w_out,
            "final_g": final_g}


def reference(x, c, positions, ada_w, ada_b, norm_g, ffn_w13, ffn_w2,
              a_w_in, a_conv_w, a_conv_b, a_b_if, a_norm_g, a_w_out,
              b_w_in, b_lam, b_norm_g, b_w_out, final_g):
    inv_freq = ROPE_THETA ** (-jnp.arange(0, ROPE_DIM, 2, dtype=jnp.float32) / ROPE_DIM)
    ang = positions.astype(jnp.float32)[..., None] * inv_freq
    cos, sin = jnp.cos(ang), jnp.sin(ang)
    cond = jax.nn.silu(c)
    for i in range(DEPTH):
        ada = (cond @ ada_w[i] + ada_b[i]).reshape(-1, 3, 3, D_MODEL)
        h = modulate(x, norm_g[i, 0], ada[:, 0, 0], ada[:, 0, 1])
        x = x + 0.5 * ada[:, 0, 2, None, :] * swiglu(h, ffn_w13[i, 0], ffn_w2[i, 0])
        h = modulate(x, norm_g[i, 1], ada[:, 1, 0], ada[:, 1, 1])
        j = i // N_MIXERS
        if i % N_MIXERS == 0:
            y = mlstm_mixer(h, a_w_in[j], a_conv_w[j], a_conv_b[j], a_b_if[j], a_norm_g[j], a_w_out[j])
        else:
            lam_init = 0.8 - 0.6 * math.exp(-0.3 * i)
            y = diff_attn_mixer(h, cos, sin, b_w_in[j], b_lam[j], b_norm_g[j], b_w_out[j], lam_init)
        x = x + ada[:, 1, 2, None, :] * y
        h = modulate(x, norm_g[i, 2], ada[:, 2, 0], ada[:, 2, 1])
        x = x + 0.5 * ada[:, 2, 2, None, :] * swiglu(h, ffn_w13[i, 1], ffn_w2[i, 1])
    return rmsnorm(x, final_g)
```

```python
import functools
import math

import jax
import jax.numpy as jnp
from jax import lax
from jax.experimental import pallas as pl
from jax.experimental.pallas import tpu as pltpu

F32 = jnp.float32
BF16 = jnp.bfloat16

EPS = 1e-6
N_MIXERS = 2
A_HEADS = 4
A_DK = 128
A_DV = 256
A_QK = A_HEADS * A_DK
A_V = A_HEADS * A_DV
A_CONV = 4
B_HEADS = 4
B_DH = 128
B_DV = 2 * B_DH
B_QK = B_HEADS * 2 * B_DH
B_V = B_HEADS * B_DV
ROPE_THETA = 500000.0
ROPE_DIM = B_DH // 4

LANES = 128
SUBLANES = 8
VMEM_LIMIT = 56 * 1024 * 1024
NEG_BIG = -1e30

FFN_TM = 512
FFN_NK = 2
PROJ_TM = 512
CELL_L = 256
ATT_TQ = 512
ATT_TK = 512


def _cparams(sem):
    return pltpu.CompilerParams(dimension_semantics=sem, vmem_limit_bytes=VMEM_LIMIT)


def _rms(x, g):
    return (x * lax.rsqrt(jnp.mean(x * x, axis=-1, keepdims=True) + EPS)) * g


def _modulate(x, g, shift, scale):
    return _rms(x, g) * (1.0 + scale) + shift


def _ada_rows(ada_ref, sub):
    return (ada_ref[0, 3 * sub:3 * sub + 1, :],
            ada_ref[0, 3 * sub + 1:3 * sub + 2, :],
            ada_ref[0, 3 * sub + 2:3 * sub + 3, :])


def _ada_kernel(c_ref, w_ref, b_ref, o_ref):
    c = c_ref[...]
    cond = (c * jax.nn.sigmoid(c)).astype(BF16)
    o_ref[0] = jnp.dot(cond, w_ref[0].astype(BF16), preferred_element_type=F32) + b_ref[0]


def _ada_all(c, ada_w, ada_b):
    depth, d, n = ada_w.shape
    nb = c.shape[0]
    rows = SUBLANES * pl.cdiv(nb, SUBLANES)
    cp = jnp.pad(c, ((0, rows - nb), (0, 0)))
    tn = 1024
    out = pl.pallas_call(
        _ada_kernel,
        grid=(depth, n // tn),
        in_specs=[pl.BlockSpec((rows, d), lambda l, j: (0, 0)),
                  pl.BlockSpec((1, d, tn), lambda l, j: (l, 0, j)),
                  pl.BlockSpec((1, 1, tn), lambda l, j: (l, 0, j))],
        out_specs=pl.BlockSpec((1, rows, tn), lambda l, j: (l, 0, j)),
        out_shape=jax.ShapeDtypeStruct((depth, rows, n), F32),
        compiler_params=_cparams(("parallel", "parallel")),
        name="ada",
    )(cp, ada_w, ada_b.reshape(depth, 1, n))
    return out[:, :nb].reshape(depth, nb, 9, d)


def _ffn_kernel(x_ref, ada_ref, g_ref, w1_ref, w3_ref, w2_ref, fg_ref, o_ref,
                h_scr, acc_scr, *, sub, final_norm):
    k = pl.program_id(2)

    @pl.when(k == 0)
    def _():
        shift, scale, _ = _ada_rows(ada_ref, sub)
        h_scr[...] = _modulate(x_ref[0], g_ref[...], shift, scale).astype(BF16)
        acc_scr[...] = jnp.zeros_like(acc_scr)

    h = h_scr[...]
    gate = jnp.dot(h, w1_ref[...], preferred_element_type=F32)
    up = jnp.dot(h, w3_ref[...], preferred_element_type=F32)
    act = ((gate * jax.nn.sigmoid(gate)) * up).astype(BF16)
    acc_scr[...] += jnp.dot(act, w2_ref[...], preferred_element_type=F32)

    @pl.when(k == pl.num_programs(2) - 1)
    def _():
        _, _, gt = _ada_rows(ada_ref, sub)
        y = x_ref[0] + (0.5 * gt) * acc_scr[...]
        if final_norm:
            y = _rms(y, fg_ref[...])
        o_ref[0] = y


def _ffn(x, ada_l, g, w13, w2, final_g, *, sub, final_norm):
    nb, s, d = x.shape
    dff = w2.shape[0]
    tm = min(FFN_TM, s)
    nk = FFN_NK
    tf = dff // nk
    assert dff % nk == 0 and tf % LANES == 0 and s % tm == 0
    return pl.pallas_call(
        functools.partial(_ffn_kernel, sub=sub, final_norm=final_norm),
        grid=(nb, s // tm, nk),
        in_specs=[pl.BlockSpec((1, tm, d), lambda b, i, k: (b, i, 0)),
                  pl.BlockSpec((1, 9, d), lambda b, i, k: (b, 0, 0)),
                  pl.BlockSpec((1, d), lambda b, i, k: (0, 0)),
                  pl.BlockSpec((d, tf), lambda b, i, k: (0, k)),
                  pl.BlockSpec((d, tf), lambda b, i, k: (0, k + nk)),
                  pl.BlockSpec((tf, d), lambda b, i, k: (k, 0)),
                  pl.BlockSpec((1, d), lambda b, i, k: (0, 0))],
        out_specs=pl.BlockSpec((1, tm, d), lambda b, i, k: (b, i, 0)),
        out_shape=jax.ShapeDtypeStruct(x.shape, F32),
        scratch_shapes=[pltpu.VMEM((tm, d), BF16), pltpu.VMEM((tm, d), F32)],
        compiler_params=_cparams(("parallel", "parallel", "arbitrary")),
        name="ffn",
    )(x, ada_l, g.reshape(1, d), w13, w13, w2, final_g.reshape(1, d))


def _outproj_kernel(x_ref, y_ref, ada_ref, w_ref, o_ref):
    _, _, gt = _ada_rows(ada_ref, 1)
    o_ref[0] = x_ref[0] + gt * jnp.dot(y_ref[0], w_ref[...], preferred_element_type=F32)


def _outproj(x, y, ada_l, w_out):
    nb, s, d = x.shape
    dv = y.shape[-1]
    tm = min(PROJ_TM, s)
    return pl.pallas_call(
        _outproj_kernel,
        grid=(nb, s // tm),
        in_specs=[pl.BlockSpec((1, tm, d), lambda b, i: (b, i, 0)),
                  pl.BlockSpec((1, tm, dv), lambda b, i: (b, i, 0)),
                  pl.BlockSpec((1, 9, d), lambda b, i: (b, 0, 0)),
                  pl.BlockSpec((dv, d), lambda b, i: (0, 0))],
        out_specs=pl.BlockSpec((1, tm, d), lambda b, i: (b, i, 0)),
        out_shape=jax.ShapeDtypeStruct(x.shape, F32),
        compiler_params=_cparams(("parallel", "parallel")),
        name="outproj",
    )(x, y, ada_l, w_out)


def _a_in_kernel(x_ref, xh_ref, ada_ref, g_ref, w_ref, wg_ref, cw_ref, cb_ref, bif_ref,
                 q_ref, k_ref, v_ref, o_ref, gates_ref, ext_scr):
    i = pl.program_id(1)
    tm = x_ref.shape[1]
    halo = xh_ref.shape[1]
    shift, scale, _ = _ada_rows(ada_ref, 1)
    g = g_ref[...]
    h = _modulate(x_ref[0], g, shift, scale).astype(BF16)
    hh = _modulate(xh_ref[0], g, shift, scale).astype(BF16)

    nqk = 2 * A_QK
    wqk = w_ref[:, 0:nqk]
    pre = jnp.dot(h, wqk, preferred_element_type=F32)
    pre_h = jnp.dot(hh, wqk, preferred_element_type=F32)
    ext_scr[0:halo, :] = jnp.where(i == 0, 0.0, pre_h)
    ext_scr[halo:halo + tm, :] = pre
    conv = cb_ref[...] + cw_ref[A_CONV - 1:A_CONV, :] * pre
    for j in range(A_CONV - 1):
        off = halo - (A_CONV - 1) + j
        conv = conv + cw_ref[j:j + 1, :] * ext_scr[off:off + tm, :]
    qk = conv * jax.nn.sigmoid(conv)
    q_ref[0] = qk[:, 0:A_QK].astype(BF16)
    k_ref[0] = (qk[:, A_QK:nqk] * (A_DK ** -0.5)).astype(BF16)

    v_ref[0] = jnp.dot(h, w_ref[:, nqk:nqk + A_V], preferred_element_type=F32).astype(BF16)
    o_ref[0] = jnp.dot(h, w_ref[:, nqk + A_V:nqk + 2 * A_V], preferred_element_type=F32)

    gp = jnp.dot(h, wg_ref[...], preferred_element_type=F32) + bif_ref[...]
    lane = lax.broadcasted_iota(jnp.int32, gp.shape, 1)
    gates_ref[0] = jnp.where(lane < A_HEADS, gp, jax.nn.log_sigmoid(gp))


def _a_in(x, ada_l, g, w_main, w_gates, conv_w, conv_b, b_if):
    nb, s, d = x.shape
    tm = min(PROJ_TM, s)
    halo = SUBLANES
    nqk = 2 * A_QK
    hb = tm // halo
    tok = lambda b, i: (b, i, 0)
    return pl.pallas_call(
        _a_in_kernel,
        grid=(nb, s // tm),
        in_specs=[pl.BlockSpec((1, tm, d), tok),
                  pl.BlockSpec((1, halo, d), lambda b, i: (b, jnp.maximum(i * hb - 1, 0), 0)),
                  pl.BlockSpec((1, 9, d), lambda b, i: (b, 0, 0)),
                  pl.BlockSpec((1, d), lambda b, i: (0, 0)),
                  pl.BlockSpec(w_main.shape, lambda b, i: (0, 0)),
                  pl.BlockSpec(w_gates.shape, lambda b, i: (0, 0)),
                  pl.BlockSpec((A_CONV, nqk), lambda b, i: (0, 0)),
                  pl.BlockSpec((1, nqk), lambda b, i: (0, 0)),
                  pl.BlockSpec((1, LANES), lambda b, i: (0, 0))],
        out_specs=[pl.BlockSpec((1, tm, A_QK), tok),
                   pl.BlockSpec((1, tm, A_QK), tok),
                   pl.BlockSpec((1, tm, A_V), tok),
                   pl.BlockSpec((1, tm, A_V), tok),
                   pl.BlockSpec((1, tm, LANES), tok)],
        out_shape=[jax.ShapeDtypeStruct((nb, s, A_QK), BF16),
                   jax.ShapeDtypeStruct((nb, s, A_QK), BF16),
                   jax.ShapeDtypeStruct((nb, s, A_V), BF16),
                   jax.ShapeDtypeStruct((nb, s, A_V), F32),
                   jax.ShapeDtypeStruct((nb, s, LANES), F32)],
        scratch_shapes=[pltpu.VMEM((halo + tm, nqk), F32)],
        compiler_params=_cparams(("parallel", "arbitrary")),
        name="mlstm_in",
    )(x, x, ada_l, g.reshape(1, d), w_main, w_gates, conv_w, conv_b.reshape(1, nqk), b_if)


def _scan_rows(x, op, fill):
    n = x.shape[0]
    row = lax.broadcasted_iota(jnp.int32, x.shape, 0)
    d = 1
    while d < n:
        x = op(x, jnp.where(row >= d, pltpu.roll(x, d, axis=0), fill))
        d *= 2
    return x


def _a_cell_kernel(q_ref, k_ref, v_ref, o_ref, gates_ref, ng_ref, y_ref,
                   c_scr, n_scr, m_scr):
    L = q_ref.shape[1]

    @pl.when(pl.program_id(1) == 0)
    def _():
        c_scr[...] = jnp.zeros_like(c_scr)
        n_scr[...] = jnp.zeros_like(n_scr)
        m_scr[...] = jnp.zeros_like(m_scr)

    gates = gates_ref[0]
    log_i = gates
    log_f = pltpu.roll(gates, LANES - A_HEADS, axis=1)
    b = _scan_rows(log_f, jnp.add, 0.0)
    g = log_i - b
    m_prev = m_scr[...]
    big_m = jnp.maximum(m_prev, _scan_rows(g, jnp.maximum, NEG_BIG))
    inter = jnp.exp(m_prev - big_m)
    m_t = b + big_m
    floor = jnp.exp(-m_t)
    m_last = big_m[L - 1:L, :]
    w_col = jnp.exp(g - m_last)
    decay = inter[L - 1:L, :]
    m_scr[...] = m_t[L - 1:L, :]
    g_t = jnp.transpose(g)

    row = lax.broadcasted_iota(jnp.int32, (L, L), 0)
    col = lax.broadcasted_iota(jnp.int32, (L, L), 1)
    causal = col <= row

    for h in range(A_HEADS):
        qh = q_ref[0, :, h * A_DK:(h + 1) * A_DK]
        kh = k_ref[0, :, h * A_DK:(h + 1) * A_DK]
        vh = v_ref[0, :, h * A_DV:(h + 1) * A_DV]
        dw = jnp.exp(jnp.where(causal, g_t[h:h + 1, :] - big_m[:, h:h + 1], NEG_BIG))
        s = lax.dot_general(qh, kh, (((1,), (1,)), ((), ())), preferred_element_type=F32) * dw
        inter_h = inter[:, h:h + 1]
        c_h = c_scr[h]
        num = inter_h * jnp.dot(qh, c_h.astype(BF16), preferred_element_type=F32) \
            + jnp.dot(s.astype(BF16), vh, preferred_element_type=F32)
        qn = jnp.sum(qh.astype(F32) * n_scr[h], axis=-1, keepdims=True)
        den = inter_h * qn + jnp.sum(s, axis=-1, keepdims=True)
        hv = num / jnp.maximum(jnp.abs(den), floor[:, h:h + 1])
        hv = hv * lax.rsqrt(jnp.mean(hv * hv, axis=-1, keepdims=True) + EPS)
        og = o_ref[0, :, h * A_DV:(h + 1) * A_DV]
        y = jax.nn.sigmoid(og) * (hv * ng_ref[:, h * A_DV:(h + 1) * A_DV])
        y_ref[0, :, h * A_DV:(h + 1) * A_DV] = y.astype(BF16)

        kw = kh.astype(F32) * w_col[:, h:h + 1]
        dec = decay[:, h:h + 1]
        c_scr[h] = dec * c_h + jnp.dot(jnp.transpose(kw).astype(BF16), vh,
                                       preferred_element_type=F32)
        n_scr[h] = dec * n_scr[h] + jnp.sum(kw, axis=0, keepdims=True)


def _a_cell(q, k, v, o, gates, norm_g):
    nb, s, _ = q.shape
    L = min(CELL_L, s)
    tok = lambda b, c: (b, c, 0)
    return pl.pallas_call(
        _a_cell_kernel,
        grid=(nb, s // L),
        in_specs=[pl.BlockSpec((1, L, A_QK), tok),
                  pl.BlockSpec((1, L, A_QK), tok),
                  pl.BlockSpec((1, L, A_V), tok),
                  pl.BlockSpec((1, L, A_V), tok),
                  pl.BlockSpec((1, L, LANES), tok),
                  pl.BlockSpec((1, A_V), lambda b, c: (0, 0))],
        out_specs=pl.BlockSpec((1, L, A_V), tok),
        out_shape=jax.ShapeDtypeStruct((nb, s, A_V), BF16),
        scratch_shapes=[pltpu.VMEM((A_HEADS, A_DK, A_DV), F32),
                        pltpu.VMEM((A_HEADS, 1, A_DK), F32),
                        pltpu.VMEM((1, LANES), F32)],
        compiler_params=_cparams(("parallel", "arbitrary")),
        name="mlstm_cell",
    )(q, k, v, o, gates, norm_g.reshape(1, A_V))


def _rope_tab_kernel(pos_ref, invf_ref, cos_ref, sin_ref):
    ang = pos_ref[0].astype(F32) * invf_ref[...]
    lane = lax.broadcasted_iota(jnp.int32, ang.shape, 1)
    half = ROPE_DIM // 2
    sn = jnp.sin(ang)
    cos_ref[0] = jnp.where(lane < ROPE_DIM, jnp.cos(ang), 1.0)
    sin_ref[0] = jnp.where(lane < half, -sn, jnp.where(lane < ROPE_DIM, sn, 0.0))


def _rope_tables(positions):
    nb, s = positions.shape
    half = ROPE_DIM // 2
    inv_freq = ROPE_THETA ** (-jnp.arange(0, ROPE_DIM, 2, dtype=F32) / ROPE_DIM)
    invf = jnp.zeros((1, LANES), F32).at[0, :ROPE_DIM].set(jnp.tile(inv_freq, 2))
    tm = min(1024, s)
    tok = lambda b, i: (b, i, 0)
    return pl.pallas_call(
        _rope_tab_kernel,
        grid=(nb, s // tm),
        in_specs=[pl.BlockSpec((1, tm, 1), tok),
                  pl.BlockSpec((1, LANES), lambda b, i: (0, 0))],
        out_specs=[pl.BlockSpec((1, tm, LANES), tok), pl.BlockSpec((1, tm, LANES), tok)],
        out_shape=[jax.ShapeDtypeStruct((nb, s, LANES), F32)] * 2,
        compiler_params=_cparams(("parallel", "parallel")),
        name="rope_tables",
    )(positions.reshape(nb, s, 1), invf)


def _rope(x, cos_t, sin_t):
    half = ROPE_DIM // 2
    lane = lax.broadcasted_iota(jnp.int32, cos_t.shape, 1)
    outs = []
    for c in range(x.shape[1] // B_DH):
        xg = x[:, c * B_DH:(c + 1) * B_DH]
        swapped = jnp.where(lane < half, pltpu.roll(xg, B_DH - half, axis=1),
                            pltpu.roll(xg, half, axis=1))
        outs.append(xg * cos_t + swapped * sin_t)
    return jnp.concatenate(outs, axis=1)


def _b_in_kernel(x_ref, ada_ref, g_ref, w_ref, cos_ref, sin_ref, q_ref, k_ref, v_ref):
    shift, scale, _ = _ada_rows(ada_ref, 1)
    h = _modulate(x_ref[0], g_ref[...], shift, scale).astype(BF16)
    cos_t = cos_ref[0]
    sin_t = sin_ref[0]
    q = jnp.dot(h, w_ref[:, 0:B_QK], preferred_element_type=F32)
    q_ref[0] = (_rope(q, cos_t, sin_t) * (B_DH ** -0.5)).astype(BF16)
    k = jnp.dot(h, w_ref[:, B_QK:2 * B_QK], preferred_element_type=F32)
    k_ref[0] = _rope(k, cos_t, sin_t).astype(BF16)
    v_ref[0] = jnp.dot(h, w_ref[:, 2 * B_QK:2 * B_QK + B_V],
                       preferred_element_type=F32).astype(BF16)


def _b_in(x, ada_l, g, w_in, cos_t, sin_t):
    nb, s, d = x.shape
    tm = min(PROJ_TM, s)
    tok = lambda b, i: (b, i, 0)
    return pl.pallas_call(
        _b_in_kernel,
        grid=(nb, s // tm),
        in_specs=[pl.BlockSpec((1, tm, d), tok),
                  pl.BlockSpec((1, 9, d), lambda b, i: (b, 0, 0)),
                  pl.BlockSpec((1, d), lambda b, i: (0, 0)),
                  pl.BlockSpec(w_in.shape, lambda b, i: (0, 0)),
                  pl.BlockSpec((1, tm, LANES), tok),
                  pl.BlockSpec((1, tm, LANES), tok)],
        out_specs=[pl.BlockSpec((1, tm, B_QK), tok),
                   pl.BlockSpec((1, tm, B_QK), tok),
                   pl.BlockSpec((1, tm, B_V), tok)],
        out_shape=[jax.ShapeDtypeStruct((nb, s, B_QK), BF16),
                   jax.ShapeDtypeStruct((nb, s, B_QK), BF16),
                   jax.ShapeDtypeStruct((nb, s, B_V), BF16)],
        compiler_params=_cparams(("parallel", "parallel")),
        name="attn_in",
    )(x, ada_l, g.reshape(1, d), w_in, cos_t, sin_t)


def _b_attn_kernel(qi_ref, kj_ref, q_ref, k_ref, v_ref, lam_ref, ng_ref, y_ref,
                   m_scr, l_scr, acc_scr, *, lam_init):
    p = pl.program_id(2)
    qi = qi_ref[p]
    kj = kj_ref[p]
    tq = q_ref.shape[1]
    tk = k_ref.shape[1]

    @pl.when(kj == 0)
    def _():
        m_scr[...] = jnp.full_like(m_scr, NEG_BIG)
        l_scr[...] = jnp.zeros_like(l_scr)
        acc_scr[...] = jnp.zeros_like(acc_scr)

    def update(masked):
        v = v_ref[0]
        if masked:
            row = lax.broadcasted_iota(jnp.int32, (tq, tk), 0)
            col = lax.broadcasted_iota(jnp.int32, (tq, tk), 1)
            keep = col <= row
        for c in range(2):
            qc = q_ref[0, :, c * B_DH:(c + 1) * B_DH]
            kc = k_ref[0, :, c * B_DH:(c + 1) * B_DH]
            s = lax.dot_general(qc, kc, (((1,), (1,)), ((), ())), preferred_element_type=F32)
            if masked:
                s = jnp.where(keep, s, NEG_BIG)
            m_old = m_scr[c]
            m_new = jnp.maximum(m_old, jnp.max(s, axis=-1, keepdims=True))
            alpha = jnp.exp(m_old - m_new)
            pr = jnp.exp(s - m_new)
            l_scr[c] = alpha * l_scr[c] + jnp.sum(pr, axis=-1, keepdims=True)
            acc_scr[c] = alpha * acc_scr[c] + jnp.dot(pr.astype(BF16), v,
                                                      preferred_element_type=F32)
            m_scr[c] = m_new

    @pl.when(kj < qi)
    def _():
        update(False)

    @pl.when(kj == qi)
    def _():
        update(True)
        lf = lam_ref[...]
        lam_full = (jnp.exp(jnp.sum(lf[0:1] * lf[1:2], axis=-1, keepdims=True))
                    - jnp.exp(jnp.sum(lf[2:3] * lf[3:4], axis=-1, keepdims=True)) + lam_init)
        o = acc_scr[0] / l_scr[0] - lam_full * (acc_scr[1] / l_scr[1])
        o = o * lax.rsqrt(jnp.mean(o * o, axis=-1, keepdims=True) + EPS)
        y_ref[0] = ((o * ng_ref[...]) * (1.0 - lam_init)).astype(BF16)


def _b_attn(q, k, v, lam, norm_g, lam_init):
    nb, s, _ = q.shape
    tq = min(ATT_TQ, s)
    tk = tq
    nblk = s // tq
    pairs = [(i, j) for i in range(nblk) for j in range(i + 1)]
    qi = jnp.array([p[0] for p in pairs], jnp.int32)
    kj = jnp.array([p[1] for p in pairs], jnp.int32)
    grid_spec = pltpu.PrefetchScalarGridSpec(
        num_scalar_prefetch=2,
        grid=(nb, B_HEADS, len(pairs)),
        in_specs=[pl.BlockSpec((1, tq, 2 * B_DH), lambda b, h, p, qi, kj: (b, qi[p], h)),
                  pl.BlockSpec((1, tk, 2 * B_DH), lambda b, h, p, qi, kj: (b, kj[p], h)),
                  pl.BlockSpec((1, tk, B_DV), lambda b, h, p, qi, kj: (b, kj[p], h)),
                  pl.BlockSpec((4, B_DH), lambda b, h, p, qi, kj: (0, 0)),
                  pl.BlockSpec((1, B_DV), lambda b, h, p, qi, kj: (0, 0))],
        out_specs=pl.BlockSpec((1, tq, B_DV), lambda b, h, p, qi, kj: (b, qi[p], h)),
        scratch_shapes=[pltpu.VMEM((2, tq, 1), F32),
                        pltpu.VMEM((2, tq, 1), F32),
                        pltpu.VMEM((2, tq, B_DV), F32)],
    )
    return pl.pallas_call(
        functools.partial(_b_attn_kernel, lam_init=lam_init),
        grid_spec=grid_spec,
        out_shape=jax.ShapeDtypeStruct((nb, s, B_V), BF16),
        compiler_params=_cparams(("parallel", "parallel", "arbitrary")),
        name="diff_attn",
    )(qi, kj, q, k, v, lam, norm_g.reshape(1, B_DV))


def kernel(x, c, positions, ada_w, ada_b, norm_g, ffn_w13, ffn_w2, a_w_in, a_conv_w, a_conv_b,
           a_b_if, a_norm_g, a_w_out, b_w_in, b_lam, b_norm_g, b_w_out, final_g):
    depth = ada_w.shape[0]
    d = x.shape[-1]
    ada = _ada_all(c, ada_w, ada_b)
    cos_t, sin_t = _rope_tables(positions)

    nqk = 2 * A_QK
    n_main = nqk + 2 * A_V
    for i in range(depth):
        ada_l = ada[i]
        x = _ffn(x, ada_l, norm_g[i, 0], ffn_w13[i, 0].astype(BF16), ffn_w2[i, 0].astype(BF16),
                 final_g, sub=0, final_norm=False)
        j = i // N_MIXERS
        if i % N_MIXERS == 0:
            w_in = a_w_in[j]
            w_main = w_in[:, :n_main].astype(BF16)
            w_gates = jnp.pad(w_in[:, n_main:], ((0, 0), (0, LANES - 2 * A_HEADS))).astype(BF16)
            b_if = jnp.pad(a_b_if[j], (0, LANES - 2 * A_HEADS)).reshape(1, LANES)
            q, k, v, o, gates = _a_in(x, ada_l, norm_g[i, 1], w_main, w_gates,
                                      a_conv_w[j], a_conv_b[j], b_if)
            y = _a_cell(q, k, v, o, gates, a_norm_g[j])
            x = _outproj(x, y, ada_l, a_w_out[j].astype(BF16))
        else:
            lam_init = 0.8 - 0.6 * math.exp(-0.3 * i)
            q, k, v = _b_in(x, ada_l, norm_g[i, 1], b_w_in[j].astype(BF16), cos_t, sin_t)
            y = _b_attn(q, k, v, b_lam[j], b_norm_g[j], lam_init)
            x = _outproj(x, y, ada_l, b_w_out[j].astype(BF16))
        x = _ffn(x, ada_l, norm_g[i, 2], ffn_w13[i, 1].astype(BF16), ffn_w2[i, 1].astype(BF16),
                 final_g, sub=2, final_norm=(i == depth - 1))
    return x
```

```python
import functools
import math

import jax
import jax.numpy as jnp
from jax import lax
from jax.experimental import pallas as pl
from jax.experimental.pallas import tpu as pltpu

F32 = jnp.float32
BF16 = jnp.bfloat16

EPS = 1e-6
N_MIXERS = 2
A_HEADS = 4
A_DK = 128
A_DV = 256
A_QK = A_HEADS * A_DK
A_V = A_HEADS * A_DV
A_CONV = 4
B_HEADS = 4
B_DH = 128
B_DV = 2 * B_DH
B_QK = B_HEADS * 2 * B_DH
B_V = B_HEADS * B_DV
ROPE_THETA = 500000.0
ROPE_DIM = B_DH // 4

LANES = 128
SUBLANES = 8
VMEM_LIMIT = 56 * 1024 * 1024
NEG_BIG = -1e30
LOG2E = math.log2(math.e)

FFN_TM = 512
FFN_NK = 2
PROJ_TM = 512
CELL_L = 256
ATT_TQ = 512
ATT_TK = 512


def _cparams(sem):
    return pltpu.CompilerParams(dimension_semantics=sem, vmem_limit_bytes=VMEM_LIMIT)


def _rms(x, g):
    return (x * lax.rsqrt(jnp.mean(x * x, axis=-1, keepdims=True) + EPS)) * g


def _modulate(x, g, shift, scale):
    return _rms(x, g) * (1.0 + scale) + shift


def _ada_rows(ada_ref, sub):
    return (ada_ref[0, 3 * sub:3 * sub + 1, :],
            ada_ref[0, 3 * sub + 1:3 * sub + 2, :],
            ada_ref[0, 3 * sub + 2:3 * sub + 3, :])


def _ada_kernel(c_ref, w_ref, b_ref, o_ref):
    c = c_ref[...]
    cond = (c * jax.nn.sigmoid(c)).astype(BF16)
    o_ref[0] = jnp.dot(cond, w_ref[0].astype(BF16), preferred_element_type=F32) + b_ref[0]


def _ada_all(c, ada_w, ada_b):
    depth, d, n = ada_w.shape
    nb = c.shape[0]
    rows = SUBLANES * pl.cdiv(nb, SUBLANES)
    cp = jnp.pad(c, ((0, rows - nb), (0, 0)))
    tn = 1024
    out = pl.pallas_call(
        _ada_kernel,
        grid=(depth, n // tn),
        in_specs=[pl.BlockSpec((rows, d), lambda l, j: (0, 0)),
                  pl.BlockSpec((1, d, tn), lambda l, j: (l, 0, j)),
                  pl.BlockSpec((1, 1, tn), lambda l, j: (l, 0, j))],
        out_specs=pl.BlockSpec((1, rows, tn), lambda l, j: (l, 0, j)),
        out_shape=jax.ShapeDtypeStruct((depth, rows, n), F32),
        compiler_params=_cparams(("parallel", "parallel")),
        name="ada",
    )(cp, ada_w, ada_b.reshape(depth, 1, n))
    return out[:, :nb].reshape(depth, nb, 9, d)


def _ffn_kernel(x_ref, ada_ref, g_ref, w1_ref, w3_ref, w2_ref, fg_ref, o_ref,
                h_scr, acc_scr, *, sub, final_norm):
    k = pl.program_id(2)

    @pl.when(k == 0)
    def _():
        shift, scale, _ = _ada_rows(ada_ref, sub)
        h_scr[...] = _modulate(x_ref[0], g_ref[...], shift, scale).astype(BF16)
        acc_scr[...] = jnp.zeros_like(acc_scr)

    h = h_scr[...]
    gate = jnp.dot(h, w1_ref[...], preferred_element_type=F32)
    up = jnp.dot(h, w3_ref[...], preferred_element_type=F32)
    act = ((gate * jax.nn.sigmoid(gate)) * up).astype(BF16)
    acc_scr[...] += jnp.dot(act, w2_ref[...], preferred_element_type=F32)

    @pl.when(k == pl.num_programs(2) - 1)
    def _():
        _, _, gt = _ada_rows(ada_ref, sub)
        y = x_ref[0] + (0.5 * gt) * acc_scr[...]
        if final_norm:
            y = _rms(y, fg_ref[...])
        o_ref[0] = y


def _ffn(x, ada_l, g, w13, w2, final_g, *, sub, final_norm):
    nb, s, d = x.shape
    dff = w2.shape[0]
    tm = min(FFN_TM, s)
    nk = FFN_NK
    tf = dff // nk
    assert dff % nk == 0 and tf % LANES == 0 and s % tm == 0
    return pl.pallas_call(
        functools.partial(_ffn_kernel, sub=sub, final_norm=final_norm),
        grid=(nb, s // tm, nk),
        in_specs=[pl.BlockSpec((1, tm, d), lambda b, i, k: (b, i, 0)),
                  pl.BlockSpec((1, 9, d), lambda b, i, k: (b, 0, 0)),
                  pl.BlockSpec((1, d), lambda b, i, k: (0, 0)),
                  pl.BlockSpec((d, tf), lambda b, i, k: (0, k)),
                  pl.BlockSpec((d, tf), lambda b, i, k: (0, k + nk)),
                  pl.BlockSpec((tf, d), lambda b, i, k: (k, 0)),
                  pl.BlockSpec((1, d), lambda b, i, k: (0, 0))],
        out_specs=pl.BlockSpec((1, tm, d), lambda b, i, k: (b, i, 0)),
        out_shape=jax.ShapeDtypeStruct(x.shape, F32),
        scratch_shapes=[pltpu.VMEM((tm, d), BF16), pltpu.VMEM((tm, d), F32)],
        compiler_params=_cparams(("parallel", "parallel", "arbitrary")),
        name="ffn",
    )(x, ada_l, g.reshape(1, d), w13, w13, w2, final_g.reshape(1, d))


def _outproj_kernel(x_ref, y_ref, ada_ref, w_ref, o_ref):
    _, _, gt = _ada_rows(ada_ref, 1)
    o_ref[0] = x_ref[0] + gt * jnp.dot(y_ref[0], w_ref[...], preferred_element_type=F32)


def _outproj(x, y, ada_l, w_out):
    nb, s, d = x.shape
    dv = y.shape[-1]
    tm = min(PROJ_TM, s)
    return pl.pallas_call(
        _outproj_kernel,
        grid=(nb, s // tm),
        in_specs=[pl.BlockSpec((1, tm, d), lambda b, i: (b, i, 0)),
                  pl.BlockSpec((1, tm, dv), lambda b, i: (b, i, 0)),
                  pl.BlockSpec((1, 9, d), lambda b, i: (b, 0, 0)),
                  pl.BlockSpec((dv, d), lambda b, i: (0, 0))],
        out_specs=pl.BlockSpec((1, tm, d), lambda b, i: (b, i, 0)),
        out_shape=jax.ShapeDtypeStruct(x.shape, F32),
        compiler_params=_cparams(("parallel", "parallel")),
        name="outproj",
    )(x, y, ada_l, w_out)


def _a_in_kernel(x_ref, xh_ref, ada_ref, g_ref, w_ref, wg_ref, cw_ref, cb_ref, bif_ref,
                 q_ref, k_ref, v_ref, o_ref, gates_ref, ext_scr):
    i = pl.program_id(1)
    tm = x_ref.shape[1]
    halo = xh_ref.shape[1]
    shift, scale, _ = _ada_rows(ada_ref, 1)
    g = g_ref[...]
    h = _modulate(x_ref[0], g, shift, scale).astype(BF16)
    hh = _modulate(xh_ref[0], g, shift, scale).astype(BF16)

    nqk = 2 * A_QK
    wqk = w_ref[:, 0:nqk]
    pre = jnp.dot(h, wqk, preferred_element_type=F32)
    pre_h = jnp.dot(hh, wqk, preferred_element_type=F32)
    ext_scr[0:halo, :] = jnp.where(i == 0, 0.0, pre_h)
    ext_scr[halo:halo + tm, :] = pre
    conv = cb_ref[...] + cw_ref[A_CONV - 1:A_CONV, :] * pre
    for j in range(A_CONV - 1):
        off = halo - (A_CONV - 1) + j
        conv = conv + cw_ref[j:j + 1, :] * ext_scr[off:off + tm, :]
    qk = conv * jax.nn.sigmoid(conv)
    q_ref[0] = qk[:, 0:A_QK].astype(BF16)
    k_ref[0] = (qk[:, A_QK:nqk] * (A_DK ** -0.5)).astype(BF16)

    v_ref[0] = jnp.dot(h, w_ref[:, nqk:nqk + A_V], preferred_element_type=F32).astype(BF16)
    o_ref[0] = jnp.dot(h, w_ref[:, nqk + A_V:nqk + 2 * A_V], preferred_element_type=F32)

    gp = jnp.dot(h, wg_ref[...], preferred_element_type=F32) + bif_ref[...]
    lane = lax.broadcasted_iota(jnp.int32, gp.shape, 1)
    gates_ref[0] = jnp.where(lane < A_HEADS, gp, jax.nn.log_sigmoid(gp))


def _a_in(x, ada_l, g, w_main, w_gates, conv_w, conv_b, b_if):
    nb, s, d = x.shape
    tm = min(PROJ_TM, s)
    halo = SUBLANES
    nqk = 2 * A_QK
    hb = tm // halo
    tok = lambda b, i: (b, i, 0)
    return pl.pallas_call(
        _a_in_kernel,
        grid=(nb, s // tm),
        in_specs=[pl.BlockSpec((1, tm, d), tok),
                  pl.BlockSpec((1, halo, d), lambda b, i: (b, jnp.maximum(i * hb - 1, 0), 0)),
                  pl.BlockSpec((1, 9, d), lambda b, i: (b, 0, 0)),
                  pl.BlockSpec((1, d), lambda b, i: (0, 0)),
                  pl.BlockSpec(w_main.shape, lambda b, i: (0, 0)),
                  pl.BlockSpec(w_gates.shape, lambda b, i: (0, 0)),
                  pl.BlockSpec((A_CONV, nqk), lambda b, i: (0, 0)),
                  pl.BlockSpec((1, nqk), lambda b, i: (0, 0)),
                  pl.BlockSpec((1, LANES), lambda b, i: (0, 0))],
        out_specs=[pl.BlockSpec((1, tm, A_QK), tok),
                   pl.BlockSpec((1, tm, A_QK), tok),
                   pl.BlockSpec((1, tm, A_V), tok),
                   pl.BlockSpec((1, tm, A_V), tok),
                   pl.BlockSpec((1, tm, LANES), tok)],
        out_shape=[jax.ShapeDtypeStruct((nb, s, A_QK), BF16),
                   jax.ShapeDtypeStruct((nb, s, A_QK), BF16),
                   jax.ShapeDtypeStruct((nb, s, A_V), BF16),
                   jax.ShapeDtypeStruct((nb, s, A_V), F32),
                   jax.ShapeDtypeStruct((nb, s, LANES), F32)],
        scratch_shapes=[pltpu.VMEM((halo + tm, nqk), F32)],
        compiler_params=_cparams(("parallel", "arbitrary")),
        name="mlstm_in",
    )(x, x, ada_l, g.reshape(1, d), w_main, w_gates, conv_w, conv_b.reshape(1, nqk), b_if)


def _scan_rows(x, op, fill):
    n = x.shape[0]
    row = lax.broadcasted_iota(jnp.int32, x.shape, 0)
    d = 1
    while d < n:
        x = op(x, jnp.where(row >= d, pltpu.roll(x, d, axis=0), fill))
        d *= 2
    return x


def _a_cell_kernel(q_ref, k_ref, v_ref, o_ref, gates_ref, ng_ref, y_ref,
                   c_scr, n_scr, m_scr):
    L = q_ref.shape[1]

    @pl.when(pl.program_id(1) == 0)
    def _():
        c_scr[...] = jnp.zeros_like(c_scr)
        n_scr[...] = jnp.zeros_like(n_scr)
        m_scr[...] = jnp.zeros_like(m_scr)

    gates = gates_ref[0]
    log_i = gates
    log_f = pltpu.roll(gates, LANES - A_HEADS, axis=1)
    b = _scan_rows(log_f, jnp.add, 0.0)
    g = log_i - b
    m_prev = m_scr[...]
    big_m = jnp.maximum(m_prev, _scan_rows(g, jnp.maximum, NEG_BIG))
    inter = jnp.exp(m_prev - big_m)
    m_t = b + big_m
    floor = jnp.exp(-m_t)
    m_last = big_m[L - 1:L, :]
    w_col = jnp.exp(g - m_last)
    decay = inter[L - 1:L, :]
    m_scr[...] = m_t[L - 1:L, :]
    g_t = jnp.transpose(g)

    row = lax.broadcasted_iota(jnp.int32, (L, L), 0)
    col = lax.broadcasted_iota(jnp.int32, (L, L), 1)
    causal = col <= row

    for h in range(A_HEADS):
        qh = q_ref[0, :, h * A_DK:(h + 1) * A_DK]
        kh = k_ref[0, :, h * A_DK:(h + 1) * A_DK]
        vh = v_ref[0, :, h * A_DV:(h + 1) * A_DV]
        dw = jnp.exp(jnp.where(causal, g_t[h:h + 1, :] - big_m[:, h:h + 1], NEG_BIG))
        s = lax.dot_general(qh, kh, (((1,), (1,)), ((), ())), preferred_element_type=F32) * dw
        inter_h = inter[:, h:h + 1]
        c_h = c_scr[h]
        num = inter_h * jnp.dot(qh, c_h.astype(BF16), preferred_element_type=F32) \
            + jnp.dot(s.astype(BF16), vh, preferred_element_type=F32)
        qn = jnp.sum(qh.astype(F32) * n_scr[h], axis=-1, keepdims=True)
        den = inter_h * qn + jnp.sum(s, axis=-1, keepdims=True)
        hv = num / jnp.maximum(jnp.abs(den), floor[:, h:h + 1])
        hv = hv * lax.rsqrt(jnp.mean(hv * hv, axis=-1, keepdims=True) + EPS)
        og = o_ref[0, :, h * A_DV:(h + 1) * A_DV]
        y = jax.nn.sigmoid(og) * (hv * ng_ref[:, h * A_DV:(h + 1) * A_DV])
        y_ref[0, :, h * A_DV:(h + 1) * A_DV] = y.astype(BF16)

        kw = kh.astype(F32) * w_col[:, h:h + 1]
        dec = decay[:, h:h + 1]
        c_scr[h] = dec * c_h + jnp.dot(jnp.transpose(kw).astype(BF16), vh,
                                       preferred_element_type=F32)
        n_scr[h] = dec * n_scr[h] + jnp.sum(kw, axis=0, keepdims=True)


def _a_cell(q, k, v, o, gates, norm_g):
    nb, s, _ = q.shape
    L = min(CELL_L, s)
    tok = lambda b, c: (b, c, 0)
    return pl.pallas_call(
        _a_cell_kernel,
        grid=(nb, s // L),
        in_specs=[pl.BlockSpec((1, L, A_QK), tok),
                  pl.BlockSpec((1, L, A_QK), tok),
                  pl.BlockSpec((1, L, A_V), tok),
                  pl.BlockSpec((1, L, A_V), tok),
                  pl.BlockSpec((1, L, LANES), tok),
                  pl.BlockSpec((1, A_V), lambda b, c: (0, 0))],
        out_specs=pl.BlockSpec((1, L, A_V), tok),
        out_shape=jax.ShapeDtypeStruct((nb, s, A_V), BF16),
        scratch_shapes=[pltpu.VMEM((A_HEADS, A_DK, A_DV), F32),
                        pltpu.VMEM((A_HEADS, 1, A_DK), F32),
                        pltpu.VMEM((1, LANES), F32)],
        compiler_params=_cparams(("parallel", "arbitrary")),
        name="mlstm_cell",
    )(q, k, v, o, gates, norm_g.reshape(1, A_V))


def _rope_tab_kernel(pos_ref, invf_ref, cos_ref, sin_ref):
    ang = pos_ref[0].astype(F32) * invf_ref[...]
    lane = lax.broadcasted_iota(jnp.int32, ang.shape, 1)
    half = ROPE_DIM // 2
    sn = jnp.sin(ang)
    cos_ref[0] = jnp.where(lane < ROPE_DIM, jnp.cos(ang), 1.0)
    sin_ref[0] = jnp.where(lane < half, -sn, jnp.where(lane < ROPE_DIM, sn, 0.0))


def _rope_tables(positions):
    nb, s = positions.shape
    half = ROPE_DIM // 2
    inv_freq = ROPE_THETA ** (-jnp.arange(0, ROPE_DIM, 2, dtype=F32) / ROPE_DIM)
    invf = jnp.zeros((1, LANES), F32).at[0, :ROPE_DIM].set(jnp.tile(inv_freq, 2))
    tm = min(1024, s)
    tok = lambda b, i: (b, i, 0)
    return pl.pallas_call(
        _rope_tab_kernel,
        grid=(nb, s // tm),
        in_specs=[pl.BlockSpec((1, tm, 1), tok),
                  pl.BlockSpec((1, LANES), lambda b, i: (0, 0))],
        out_specs=[pl.BlockSpec((1, tm, LANES), tok), pl.BlockSpec((1, tm, LANES), tok)],
        out_shape=[jax.ShapeDtypeStruct((nb, s, LANES), F32)] * 2,
        compiler_params=_cparams(("parallel", "parallel")),
        name="rope_tables",
    )(positions.reshape(nb, s, 1), invf)


def _rope(x, cos_t, sin_t):
    half = ROPE_DIM // 2
    lane = lax.broadcasted_iota(jnp.int32, cos_t.shape, 1)
    outs = []
    for c in range(x.shape[1] // B_DH):
        xg = x[:, c * B_DH:(c + 1) * B_DH]
        swapped = jnp.where(lane < half, pltpu.roll(xg, B_DH - half, axis=1),
                            pltpu.roll(xg, half, axis=1))
        outs.append(xg * cos_t + swapped * sin_t)
    return jnp.concatenate(outs, axis=1)


def _b_in_kernel(x_ref, ada_ref, g_ref, w_ref, cos_ref, sin_ref, q_ref, k_ref, v_ref):
    shift, scale, _ = _ada_rows(ada_ref, 1)
    h = _modulate(x_ref[0], g_ref[...], shift, scale).astype(BF16)
    cos_t = cos_ref[0]
    sin_t = sin_ref[0]
    q = jnp.dot(h, w_ref[:, 0:B_QK], preferred_element_type=F32)
    q_ref[0] = (_rope(q, cos_t, sin_t) * (B_DH ** -0.5 * LOG2E)).astype(BF16)
    k = jnp.dot(h, w_ref[:, B_QK:2 * B_QK], preferred_element_type=F32)
    k_ref[0] = _rope(k, cos_t, sin_t).astype(BF16)
    v_ref[0] = jnp.dot(h, w_ref[:, 2 * B_QK:2 * B_QK + B_V],
                       preferred_element_type=F32).astype(BF16)


def _b_in(x, ada_l, g, w_in, cos_t, sin_t):
    nb, s, d = x.shape
    tm = min(PROJ_TM, s)
    tok = lambda b, i: (b, i, 0)
    return pl.pallas_call(
        _b_in_kernel,
        grid=(nb, s // tm),
        in_specs=[pl.BlockSpec((1, tm, d), tok),
                  pl.BlockSpec((1, 9, d), lambda b, i: (b, 0, 0)),
                  pl.BlockSpec((1, d), lambda b, i: (0, 0)),
                  pl.BlockSpec(w_in.shape, lambda b, i: (0, 0)),
                  pl.BlockSpec((1, tm, LANES), tok),
                  pl.BlockSpec((1, tm, LANES), tok)],
        out_specs=[pl.BlockSpec((1, tm, B_QK), tok),
                   pl.BlockSpec((1, tm, B_QK), tok),
                   pl.BlockSpec((1, tm, B_V), tok)],
        out_shape=[jax.ShapeDtypeStruct((nb, s, B_QK), BF16),
                   jax.ShapeDtypeStruct((nb, s, B_QK), BF16),
                   jax.ShapeDtypeStruct((nb, s, B_V), BF16)],
        compiler_params=_cparams(("parallel", "parallel")),
        name="attn_in",
    )(x, ada_l, g.reshape(1, d), w_in, cos_t, sin_t)


def _b_attn_kernel(q_ref, k_ref, v_ref, lam_ref, ng_ref, y_ref, m_scr, l_scr, acc_scr,
                   sa_scr, sb_scr, *, lam_init, tk):
    qi = pl.program_id(2)
    tq = q_ref.shape[1]
    nchunk = tk // LANES

    m_scr[...] = jnp.full_like(m_scr, NEG_BIG)
    l_scr[...] = jnp.zeros_like(l_scr)
    acc_scr[...] = jnp.zeros_like(acc_scr)

    def scores(j, dst):
        ks = pl.multiple_of(j * tk, tk)
        for c in range(2):
            qc = q_ref[0, :, c * B_DH:(c + 1) * B_DH]
            kc = k_ref[0, pl.ds(ks, tk), c * B_DH:(c + 1) * B_DH]
            dst[c] = lax.dot_general(qc, kc, (((1,), (1,)), ((), ())),
                                     preferred_element_type=F32)

    def update(j, src, masked):
        ks = pl.multiple_of(j * tk, tk)
        v = v_ref[0, pl.ds(ks, tk), :]
        if masked:
            row = lax.broadcasted_iota(jnp.int32, (tq, tk), 0)
            col = lax.broadcasted_iota(jnp.int32, (tq, tk), 1)
            keep = col <= row
        for c in range(2):
            s = src[c]
            if masked:
                s = jnp.where(keep, s, NEG_BIG)
            m_old = m_scr[c]
            m_new = jnp.maximum(m_old, jnp.max(s, axis=-1, keepdims=True))
            alpha = jnp.exp2(m_old - m_new)
            chunks = [jnp.exp2(s[:, n * LANES:(n + 1) * LANES] - m_new) for n in range(nchunk)]
            part = chunks[0]
            for n in range(1, nchunk):
                part = part + chunks[n]
            l_scr[c] = alpha * l_scr[c] + part
            pr = jnp.concatenate(chunks, axis=1).astype(BF16)
            alpha_v = jnp.concatenate([alpha] * (B_DV // LANES), axis=1)
            acc_scr[c] = alpha_v * acc_scr[c] + jnp.dot(pr, v, preferred_element_type=F32)
            m_scr[c] = m_new

    scores(0, sa_scr)

    def body(i, carry):
        j = 2 * i
        scores(j + 1, sb_scr)
        update(j, sa_scr, False)
        scores(j + 2, sa_scr)
        update(j + 1, sb_scr, False)
        return carry

    lax.fori_loop(0, qi // 2, body, 0)

    @pl.when(qi % 2 == 1)
    def _():
        scores(qi, sb_scr)
        update(qi - 1, sa_scr, False)
        update(qi, sb_scr, True)

    @pl.when(qi % 2 == 0)
    def _():
        update(qi, sa_scr, True)

    lf = lam_ref[...]
    lam_full = (jnp.exp(jnp.sum(lf[0:1] * lf[1:2], axis=-1, keepdims=True))
                - jnp.exp(jnp.sum(lf[2:3] * lf[3:4], axis=-1, keepdims=True)) + lam_init)
    l0 = jnp.sum(l_scr[0], axis=-1, keepdims=True)
    l1 = jnp.sum(l_scr[1], axis=-1, keepdims=True)
    o = acc_scr[0] / l0 - lam_full * (acc_scr[1] / l1)
    o = o * lax.rsqrt(jnp.mean(o * o, axis=-1, keepdims=True) + EPS)
    y_ref[0] = ((o * ng_ref[...]) * (1.0 - lam_init)).astype(BF16)


def _b_attn(q, k, v, lam, norm_g, lam_init):
    nb, s, _ = q.shape
    tq = min(ATT_TQ, s)
    tk = tq
    return pl.pallas_call(
        functools.partial(_b_attn_kernel, lam_init=lam_init, tk=tk),
        grid=(nb, B_HEADS, s // tq),
        in_specs=[pl.BlockSpec((1, tq, 2 * B_DH), lambda b, h, i: (b, i, h)),
                  pl.BlockSpec((1, s, 2 * B_DH), lambda b, h, i: (b, 0, h)),
                  pl.BlockSpec((1, s, B_DV), lambda b, h, i: (b, 0, h)),
                  pl.BlockSpec((4, B_DH), lambda b, h, i: (0, 0)),
                  pl.BlockSpec((1, B_DV), lambda b, h, i: (0, 0))],
        out_specs=pl.BlockSpec((1, tq, B_DV), lambda b, h, i: (b, i, h)),
        out_shape=jax.ShapeDtypeStruct((nb, s, B_V), BF16),
        scratch_shapes=[pltpu.VMEM((2, tq, LANES), F32),
                        pltpu.VMEM((2, tq, LANES), F32),
                        pltpu.VMEM((2, tq, B_DV), F32),
                        pltpu.VMEM((2, tq, tk), F32),
                        pltpu.VMEM((2, tq, tk), F32)],
        compiler_params=_cparams(("parallel", "parallel", "arbitrary")),
        name="diff_attn",
    )(q, k, v, lam, norm_g.reshape(1, B_DV))


def kernel(x, c, positions, ada_w, ada_b, norm_g, ffn_w13, ffn_w2, a_w_in, a_conv_w, a_conv_b,
           a_b_if, a_norm_g, a_w_out, b_w_in, b_lam, b_norm_g, b_w_out, final_g):
    depth = ada_w.shape[0]
    d = x.shape[-1]
    ada = _ada_all(c, ada_w, ada_b)
    cos_t, sin_t = _rope_tables(positions)

    nqk = 2 * A_QK
    n_main = nqk + 2 * A_V
    for i in range(depth):
        ada_l = ada[i]
        x = _ffn(x, ada_l, norm_g[i, 0], ffn_w13[i, 0].astype(BF16), ffn_w2[i, 0].astype(BF16),
                 final_g, sub=0, final_norm=False)
        j = i // N_MIXERS
        if i % N_MIXERS == 0:
            w_in = a_w_in[j]
            w_main = w_in[:, :n_main].astype(BF16)
            w_gates = jnp.pad(w_in[:, n_main:], ((0, 0), (0, LANES - 2 * A_HEADS))).astype(BF16)
            b_if = jnp.pad(a_b_if[j], (0, LANES - 2 * A_HEADS)).reshape(1, LANES)
            q, k, v, o, gates = _a_in(x, ada_l, norm_g[i, 1], w_main, w_gates,
                                      a_conv_w[j], a_conv_b[j], b_if)
            y = _a_cell(q, k, v, o, gates, a_norm_g[j])
            x = _outproj(x, y, ada_l, a_w_out[j].astype(BF16))
        else:
            lam_init = 0.8 - 0.6 * math.exp(-0.3 * i)
            q, k, v = _b_in(x, ada_l, norm_g[i, 1], b_w_in[j].astype(BF16), cos_t, sin_t)
            y = _b_attn(q, k, v, b_lam[j], b_norm_g[j], lam_init)
            x = _outproj(x, y, ada_l, b_w_out[j].astype(BF16))
        x = _ffn(x, ada_l, norm_g[i, 2], ffn_w13[i, 1].astype(BF16), ffn_w2[i, 1].astype(BF16),
                 final_g, sub=2, final_norm=(i == depth - 1))
    return x
```

```python
import functools
import math

import jax
import jax.numpy as jnp
from jax import lax
from jax.experimental import pallas as pl
from jax.experimental.pallas import tpu as pltpu

F32 = jnp.float32
BF16 = jnp.bfloat16

EPS = 1e-6
N_MIXERS = 2
A_HEADS = 4
A_DK = 128
A_DV = 256
A_QK = A_HEADS * A_DK
A_V = A_HEADS * A_DV
A_CONV = 4
B_HEADS = 4
B_DH = 128
B_DV = 2 * B_DH
B_QK = B_HEADS * 2 * B_DH
B_V = B_HEADS * B_DV
ROPE_THETA = 500000.0
ROPE_DIM = B_DH // 4

LANES = 128
SUBLANES = 8
VMEM_LIMIT = 56 * 1024 * 1024
NEG_BIG = -1e30
LOG2E = math.log2(math.e)

FFN_TM = 512
FFN_NSUB = 2
FFN_CW = 256
PROJ_TM = 512
CELL_L = 256
CELL_NCH = 2
ATT_TQ = 512
ATT_TK = 512


def _cparams(sem):
    return pltpu.CompilerParams(dimension_semantics=sem, vmem_limit_bytes=VMEM_LIMIT)


def _rms(x, g):
    return (x * lax.rsqrt(jnp.mean(x * x, axis=-1, keepdims=True) + EPS)) * g


def _modulate(x, g, shift, scale):
    return _rms(x, g) * (1.0 + scale) + shift


def _ada_rows(ada_ref, sub):
    return (ada_ref[0, 3 * sub:3 * sub + 1, :],
            ada_ref[0, 3 * sub + 1:3 * sub + 2, :],
            ada_ref[0, 3 * sub + 2:3 * sub + 3, :])


def _ada_kernel(c_ref, w_ref, b_ref, o_ref):
    c = c_ref[...]
    cond = (c * jax.nn.sigmoid(c)).astype(BF16)
    o_ref[0] = jnp.dot(cond, w_ref[0].astype(BF16), preferred_element_type=F32) + b_ref[0]


def _ada_all(c, ada_w, ada_b):
    depth, d, n = ada_w.shape
    nb = c.shape[0]
    rows = SUBLANES * pl.cdiv(nb, SUBLANES)
    cp = jnp.pad(c, ((0, rows - nb), (0, 0)))
    tn = 1024
    out = pl.pallas_call(
        _ada_kernel,
        grid=(depth, n // tn),
        in_specs=[pl.BlockSpec((rows, d), lambda l, j: (0, 0)),
                  pl.BlockSpec((1, d, tn), lambda l, j: (l, 0, j)),
                  pl.BlockSpec((1, 1, tn), lambda l, j: (l, 0, j))],
        out_specs=pl.BlockSpec((1, rows, tn), lambda l, j: (l, 0, j)),
        out_shape=jax.ShapeDtypeStruct((depth, rows, n), F32),
        compiler_params=_cparams(("parallel", "parallel")),
        name="ada",
    )(cp, ada_w, ada_b.reshape(depth, 1, n))
    return out[:, :nb].reshape(depth, nb, 9, d)


def _ffn_kernel(*refs, sub, final_norm, fuse_mix, nsub):
    if fuse_mix:
        x_ref, y_ref, wo_ref, ada_ref, g_ref, w1_ref, w3_ref, w2_ref, fg_ref, o_ref, act_scr = refs
    else:
        x_ref, ada_ref, g_ref, w1_ref, w3_ref, w2_ref, fg_ref, o_ref, act_scr = refs
    tm = x_ref.shape[1]
    dff = w2_ref.shape[0]
    tr = tm // nsub
    shift, scale, gt = _ada_rows(ada_ref, sub)
    for r in range(nsub):
        rows = slice(r * tr, (r + 1) * tr)
        x = x_ref[0, rows, :]
        if fuse_mix:
            _, _, gt_mix = _ada_rows(ada_ref, 1)
            x = x + gt_mix * jnp.dot(y_ref[0, rows, :], wo_ref[...], preferred_element_type=F32)
        o_ref[0, rows, :] = x
        h = _modulate(x, g_ref[...], shift, scale).astype(BF16)
        for c in range(dff // FFN_CW):
            cols = slice(c * FFN_CW, (c + 1) * FFN_CW)
            gate = jnp.dot(h, w1_ref[:, cols], preferred_element_type=F32)
            up = jnp.dot(h, w3_ref[:, cols], preferred_element_type=F32)
            act_scr[rows, cols] = ((gate * jax.nn.sigmoid(gate)) * up).astype(BF16)
        acc = jnp.dot(act_scr[rows, :], w2_ref[...], preferred_element_type=F32)
        y = o_ref[0, rows, :] + (0.5 * gt) * acc
        if final_norm:
            y = _rms(y, fg_ref[...])
        o_ref[0, rows, :] = y


def _ffn(x, ada_l, g, w13, w2, final_g, *, sub, final_norm, mix=None):
    nb, s, d = x.shape
    dff = w2.shape[0]
    tm = min(FFN_TM, s)
    assert dff % FFN_CW == 0 and s % tm == 0
    tok = lambda b, i: (b, i, 0)
    const = lambda b, i: (0, 0)
    once = dict(pipeline_mode=pl.Buffered(1))
    in_specs = [pl.BlockSpec((1, tm, d), tok)]
    args = [x]
    if mix is not None:
        y, w_out = mix
        dv = y.shape[-1]
        in_specs += [pl.BlockSpec((1, tm, dv), tok), pl.BlockSpec((dv, d), const, **once)]
        args += [y, w_out]
    in_specs += [pl.BlockSpec((1, 9, d), lambda b, i: (b, 0, 0)),
                 pl.BlockSpec((1, d), const),
                 pl.BlockSpec((d, dff), lambda b, i: (0, 0), **once),
                 pl.BlockSpec((d, dff), lambda b, i: (0, 1), **once),
                 pl.BlockSpec((dff, d), const, **once),
                 pl.BlockSpec((1, d), const)]
    args += [ada_l, g.reshape(1, d), w13, w13, w2, final_g.reshape(1, d)]
    return pl.pallas_call(
        functools.partial(_ffn_kernel, sub=sub, final_norm=final_norm,
                          fuse_mix=mix is not None, nsub=FFN_NSUB),
        grid=(nb, s // tm),
        in_specs=in_specs,
        out_specs=pl.BlockSpec((1, tm, d), tok),
        out_shape=jax.ShapeDtypeStruct(x.shape, F32),
        scratch_shapes=[pltpu.VMEM((tm, dff), BF16)],
        compiler_params=_cparams(("parallel", "arbitrary")),
        name="ffn",
    )(*args)


def _a_in_kernel(x_ref, xh_ref, ada_ref, g_ref, w_ref, wg_ref, cw_ref, cb_ref, bif_ref,
                 q_ref, k_ref, v_ref, o_ref, gates_ref, ext_scr):
    i = pl.program_id(1)
    tm = x_ref.shape[1]
    halo = xh_ref.shape[1]
    shift, scale, _ = _ada_rows(ada_ref, 1)
    g = g_ref[...]
    h = _modulate(x_ref[0], g, shift, scale).astype(BF16)
    hh = _modulate(xh_ref[0], g, shift, scale).astype(BF16)

    nqk = 2 * A_QK
    wqk = w_ref[:, 0:nqk]
    pre = jnp.dot(h, wqk, preferred_element_type=F32)
    pre_h = jnp.dot(hh, wqk, preferred_element_type=F32)
    ext_scr[0:halo, :] = jnp.where(i == 0, 0.0, pre_h)
    ext_scr[halo:halo + tm, :] = pre
    conv = cb_ref[...] + cw_ref[A_CONV - 1:A_CONV, :] * pre
    for j in range(A_CONV - 1):
        off = halo - (A_CONV - 1) + j
        conv = conv + cw_ref[j:j + 1, :] * ext_scr[off:off + tm, :]
    qk = conv * jax.nn.sigmoid(conv)
    q_ref[0] = qk[:, 0:A_QK].astype(BF16)
    k_ref[0] = (qk[:, A_QK:nqk] * (A_DK ** -0.5)).astype(BF16)

    v_ref[0] = jnp.dot(h, w_ref[:, nqk:nqk + A_V], preferred_element_type=F32).astype(BF16)
    o_ref[0] = jnp.dot(h, w_ref[:, nqk + A_V:nqk + 2 * A_V], preferred_element_type=F32)

    gp = jnp.dot(h, wg_ref[...], preferred_element_type=F32) + bif_ref[...]
    lane = lax.broadcasted_iota(jnp.int32, gp.shape, 1)
    gates_ref[0] = jnp.where(lane < A_HEADS, gp, jax.nn.log_sigmoid(gp))


def _a_in(x, ada_l, g, w_main, w_gates, conv_w, conv_b, b_if):
    nb, s, d = x.shape
    tm = min(PROJ_TM, s)
    halo = SUBLANES
    nqk = 2 * A_QK
    hb = tm // halo
    tok = lambda b, i: (b, i, 0)
    return pl.pallas_call(
        _a_in_kernel,
        grid=(nb, s // tm),
        in_specs=[pl.BlockSpec((1, tm, d), tok),
                  pl.BlockSpec((1, halo, d), lambda b, i: (b, jnp.maximum(i * hb - 1, 0), 0)),
                  pl.BlockSpec((1, 9, d), lambda b, i: (b, 0, 0)),
                  pl.BlockSpec((1, d), lambda b, i: (0, 0)),
                  pl.BlockSpec(w_main.shape, lambda b, i: (0, 0)),
                  pl.BlockSpec(w_gates.shape, lambda b, i: (0, 0)),
                  pl.BlockSpec((A_CONV, nqk), lambda b, i: (0, 0)),
                  pl.BlockSpec((1, nqk), lambda b, i: (0, 0)),
                  pl.BlockSpec((1, LANES), lambda b, i: (0, 0))],
        out_specs=[pl.BlockSpec((1, tm, A_QK), tok),
                   pl.BlockSpec((1, tm, A_QK), tok),
                   pl.BlockSpec((1, tm, A_V), tok),
                   pl.BlockSpec((1, tm, A_V), tok),
                   pl.BlockSpec((1, tm, LANES), tok)],
        out_shape=[jax.ShapeDtypeStruct((nb, s, A_QK), BF16),
                   jax.ShapeDtypeStruct((nb, s, A_QK), BF16),
                   jax.ShapeDtypeStruct((nb, s, A_V), BF16),
                   jax.ShapeDtypeStruct((nb, s, A_V), F32),
                   jax.ShapeDtypeStruct((nb, s, LANES), F32)],
        scratch_shapes=[pltpu.VMEM((halo + tm, nqk), F32)],
        compiler_params=_cparams(("parallel", "arbitrary")),
        name="mlstm_in",
    )(x, x, ada_l, g.reshape(1, d), w_main, w_gates, conv_w, conv_b.reshape(1, nqk), b_if)


def _scan_rows(x, op, fill):
    n = x.shape[0]
    row = lax.broadcasted_iota(jnp.int32, x.shape, 0)
    d = 1
    while d < n:
        x = op(x, jnp.where(row >= d, pltpu.roll(x, d, axis=0), fill))
        d *= 2
    return x


def _a_cell_kernel(q_ref, k_ref, v_ref, o_ref, gates_ref, ng_ref, y_ref,
                   c_scr, n_scr, m_scr, *, L):

    @pl.when(pl.program_id(1) == 0)
    def _():
        c_scr[...] = jnp.zeros_like(c_scr)
        n_scr[...] = jnp.zeros_like(n_scr)
        m_scr[...] = jnp.zeros_like(m_scr)

    row = lax.broadcasted_iota(jnp.int32, (L, L), 0)
    col = lax.broadcasted_iota(jnp.int32, (L, L), 1)
    causal = col <= row

    for r in range(q_ref.shape[1] // L):
        rows = slice(r * L, (r + 1) * L)
        gates = gates_ref[0, rows, :]
        log_i = gates
        log_f = pltpu.roll(gates, LANES - A_HEADS, axis=1)
        b = _scan_rows(log_f, jnp.add, 0.0)
        g = log_i - b
        m_prev = m_scr[...]
        big_m = jnp.maximum(m_prev, _scan_rows(g, jnp.maximum, NEG_BIG))
        inter = jnp.exp(m_prev - big_m)
        m_t = b + big_m
        floor = jnp.exp(-m_t)
        m_last = big_m[L - 1:L, :]
        w_col = jnp.exp(g - m_last)
        decay = inter[L - 1:L, :]
        m_scr[...] = m_t[L - 1:L, :]
        g_t = jnp.transpose(g)

        for h in range(A_HEADS):
            qh = q_ref[0, rows, h * A_DK:(h + 1) * A_DK]
            kh = k_ref[0, rows, h * A_DK:(h + 1) * A_DK]
            vh = v_ref[0, rows, h * A_DV:(h + 1) * A_DV]
            dw = jnp.exp(jnp.where(causal, g_t[h:h + 1, :] - big_m[:, h:h + 1], NEG_BIG))
            s = lax.dot_general(qh, kh, (((1,), (1,)), ((), ())),
                                preferred_element_type=F32) * dw
            inter_h = inter[:, h:h + 1]
            c_h = c_scr[h]
            num = inter_h * jnp.dot(qh, c_h.astype(BF16), preferred_element_type=F32) \
                + jnp.dot(s.astype(BF16), vh, preferred_element_type=F32)
            qn = jnp.sum(qh.astype(F32) * n_scr[h], axis=-1, keepdims=True)
            den = inter_h * qn + jnp.sum(s, axis=-1, keepdims=True)
            hv = num / jnp.maximum(jnp.abs(den), floor[:, h:h + 1])
            hv = hv * lax.rsqrt(jnp.mean(hv * hv, axis=-1, keepdims=True) + EPS)
            og = o_ref[0, rows, h * A_DV:(h + 1) * A_DV]
            y = jax.nn.sigmoid(og) * (hv * ng_ref[:, h * A_DV:(h + 1) * A_DV])
            y_ref[0, rows, h * A_DV:(h + 1) * A_DV] = y.astype(BF16)

            kw = kh.astype(F32) * w_col[:, h:h + 1]
            dec = decay[:, h:h + 1]
            c_scr[h] = dec * c_h + jnp.dot(jnp.transpose(kw).astype(BF16), vh,
                                           preferred_element_type=F32)
            n_scr[h] = dec * n_scr[h] + jnp.sum(kw, axis=0, keepdims=True)


def _a_cell(q, k, v, o, gates, norm_g):
    nb, s, _ = q.shape
    chunk = min(CELL_L, s)
    L = min(CELL_L * CELL_NCH, s)
    tok = lambda b, c: (b, c, 0)
    return pl.pallas_call(
        functools.partial(_a_cell_kernel, L=chunk),
        grid=(nb, s // L),
        in_specs=[pl.BlockSpec((1, L, A_QK), tok),
                  pl.BlockSpec((1, L, A_QK), tok),
                  pl.BlockSpec((1, L, A_V), tok),
                  pl.BlockSpec((1, L, A_V), tok),
                  pl.BlockSpec((1, L, LANES), tok),
                  pl.BlockSpec((1, A_V), lambda b, c: (0, 0))],
        out_specs=pl.BlockSpec((1, L, A_V), tok),
        out_shape=jax.ShapeDtypeStruct((nb, s, A_V), BF16),
        scratch_shapes=[pltpu.VMEM((A_HEADS, A_DK, A_DV), F32),
                        pltpu.VMEM((A_HEADS, 1, A_DK), F32),
                        pltpu.VMEM((1, LANES), F32)],
        compiler_params=_cparams(("parallel", "arbitrary")),
        name="mlstm_cell",
    )(q, k, v, o, gates, norm_g.reshape(1, A_V))


def _rope_tab_kernel(pos_ref, invf_ref, cos_ref, sin_ref):
    ang = pos_ref[0].astype(F32) * invf_ref[...]
    lane = lax.broadcasted_iota(jnp.int32, ang.shape, 1)
    half = ROPE_DIM // 2
    sn = jnp.sin(ang)
    cos_ref[0] = jnp.where(lane < ROPE_DIM, jnp.cos(ang), 1.0)
    sin_ref[0] = jnp.where(lane < half, -sn, jnp.where(lane < ROPE_DIM, sn, 0.0))


def _rope_tables(positions):
    nb, s = positions.shape
    half = ROPE_DIM // 2
    inv_freq = ROPE_THETA ** (-jnp.arange(0, ROPE_DIM, 2, dtype=F32) / ROPE_DIM)
    invf = jnp.zeros((1, LANES), F32).at[0, :ROPE_DIM].set(jnp.tile(inv_freq, 2))
    tm = min(1024, s)
    tok = lambda b, i: (b, i, 0)
    return pl.pallas_call(
        _rope_tab_kernel,
        grid=(nb, s // tm),
        in_specs=[pl.BlockSpec((1, tm, 1), tok),
                  pl.BlockSpec((1, LANES), lambda b, i: (0, 0))],
        out_specs=[pl.BlockSpec((1, tm, LANES), tok), pl.BlockSpec((1, tm, LANES), tok)],
        out_shape=[jax.ShapeDtypeStruct((nb, s, LANES), F32)] * 2,
        compiler_params=_cparams(("parallel", "parallel")),
        name="rope_tables",
    )(positions.reshape(nb, s, 1), invf)


def _rope(x, cos_t, sin_t):
    half = ROPE_DIM // 2
    lane = lax.broadcasted_iota(jnp.int32, cos_t.shape, 1)
    outs = []
    for c in range(x.shape[1] // B_DH):
        xg = x[:, c * B_DH:(c + 1) * B_DH]
        swapped = jnp.where(lane < half, pltpu.roll(xg, B_DH - half, axis=1),
                            pltpu.roll(xg, half, axis=1))
        outs.append(xg * cos_t + swapped * sin_t)
    return jnp.concatenate(outs, axis=1)


def _b_in_kernel(x_ref, ada_ref, g_ref, w_ref, cos_ref, sin_ref, q_ref, k_ref, v_ref):
    shift, scale, _ = _ada_rows(ada_ref, 1)
    h = _modulate(x_ref[0], g_ref[...], shift, scale).astype(BF16)
    cos_t = cos_ref[0]
    sin_t = sin_ref[0]
    q = jnp.dot(h, w_ref[:, 0:B_QK], preferred_element_type=F32)
    q_ref[0] = (_rope(q, cos_t, sin_t) * (B_DH ** -0.5 * LOG2E)).astype(BF16)
    k = jnp.dot(h, w_ref[:, B_QK:2 * B_QK], preferred_element_type=F32)
    k_ref[0] = _rope(k, cos_t, sin_t).astype(BF16)
    v_ref[0] = jnp.dot(h, w_ref[:, 2 * B_QK:2 * B_QK + B_V],
                       preferred_element_type=F32).astype(BF16)


def _b_in(x, ada_l, g, w_in, cos_t, sin_t):
    nb, s, d = x.shape
    tm = min(PROJ_TM, s)
    tok = lambda b, i: (b, i, 0)
    return pl.pallas_call(
        _b_in_kernel,
        grid=(nb, s // tm),
        in_specs=[pl.BlockSpec((1, tm, d), tok),
                  pl.BlockSpec((1, 9, d), lambda b, i: (b, 0, 0)),
                  pl.BlockSpec((1, d), lambda b, i: (0, 0)),
                  pl.BlockSpec(w_in.shape, lambda b, i: (0, 0)),
                  pl.BlockSpec((1, tm, LANES), tok),
                  pl.BlockSpec((1, tm, LANES), tok)],
        out_specs=[pl.BlockSpec((1, tm, B_QK), tok),
                   pl.BlockSpec((1, tm, B_QK), tok),
                   pl.BlockSpec((1, tm, B_V), tok)],
        out_shape=[jax.ShapeDtypeStruct((nb, s, B_QK), BF16),
                   jax.ShapeDtypeStruct((nb, s, B_QK), BF16),
                   jax.ShapeDtypeStruct((nb, s, B_V), BF16)],
        compiler_params=_cparams(("parallel", "parallel")),
        name="attn_in",
    )(x, ada_l, g.reshape(1, d), w_in, cos_t, sin_t)


def _b_attn_kernel(q_ref, k_ref, v_ref, lam_ref, ng_ref, y_ref, m_scr, l_scr, acc_scr,
                   sa_scr, sb_scr, *, lam_init, tk):
    qi = pl.program_id(2)
    tq = q_ref.shape[1]
    nchunk = tk // LANES

    m_scr[...] = jnp.full_like(m_scr, NEG_BIG)
    l_scr[...] = jnp.zeros_like(l_scr)
    acc_scr[...] = jnp.zeros_like(acc_scr)

    def scores(j, dst):
        ks = pl.multiple_of(j * tk, tk)
        for c in range(2):
            qc = q_ref[0, :, c * B_DH:(c + 1) * B_DH]
            kc = k_ref[0, pl.ds(ks, tk), c * B_DH:(c + 1) * B_DH]
            dst[c] = lax.dot_general(qc, kc, (((1,), (1,)), ((), ())),
                                     preferred_element_type=F32)

    def update(j, src, masked):
        ks = pl.multiple_of(j * tk, tk)
        v = v_ref[0, pl.ds(ks, tk), :]
        if masked:
            row = lax.broadcasted_iota(jnp.int32, (tq, tk), 0)
            col = lax.broadcasted_iota(jnp.int32, (tq, tk), 1)
            keep = col <= row
        for c in range(2):
            s = src[c]
            if masked:
                s = jnp.where(keep, s, NEG_BIG)
            m_old = m_scr[c]
            m_new = jnp.maximum(m_old, jnp.max(s, axis=-1, keepdims=True))
            alpha = jnp.exp2(m_old - m_new)
            chunks = [jnp.exp2(s[:, n * LANES:(n + 1) * LANES] - m_new) for n in range(nchunk)]
            part = chunks[0]
            for n in range(1, nchunk):
                part = part + chunks[n]
            l_scr[c] = alpha * l_scr[c] + part
            pr = jnp.concatenate(chunks, axis=1).astype(BF16)
            alpha_v = jnp.concatenate([alpha] * (B_DV // LANES), axis=1)
            acc_scr[c] = alpha_v * acc_scr[c] + jnp.dot(pr, v, preferred_element_type=F32)
            m_scr[c] = m_new

    scores(0, sa_scr)

    def body(i, carry):
        j = 2 * i
        scores(j + 1, sb_scr)
        update(j, sa_scr, False)
        scores(j + 2, sa_scr)
        update(j + 1, sb_scr, False)
        return carry

    lax.fori_loop(0, qi // 2, body, 0)

    @pl.when(qi % 2 == 1)
    def _():
        scores(qi, sb_scr)
        update(qi - 1, sa_scr, False)
        update(qi, sb_scr, True)

    @pl.when(qi % 2 == 0)
    def _():
        update(qi, sa_scr, True)

    lf = lam_ref[...]
    lam_full = (jnp.exp(jnp.sum(lf[0:1] * lf[1:2], axis=-1, keepdims=True))
                - jnp.exp(jnp.sum(lf[2:3] * lf[3:4], axis=-1, keepdims=True)) + lam_init)
    l0 = jnp.sum(l_scr[0], axis=-1, keepdims=True)
    l1 = jnp.sum(l_scr[1], axis=-1, keepdims=True)
    o = acc_scr[0] / l0 - lam_full * (acc_scr[1] / l1)
    o = o * lax.rsqrt(jnp.mean(o * o, axis=-1, keepdims=True) + EPS)
    y_ref[0] = ((o * ng_ref[...]) * (1.0 - lam_init)).astype(BF16)


def _b_attn(q, k, v, lam, norm_g, lam_init):
    nb, s, _ = q.shape
    tq = min(ATT_TQ, s)
    tk = tq
    return pl.pallas_call(
        functools.partial(_b_attn_kernel, lam_init=lam_init, tk=tk),
        grid=(nb, B_HEADS, s // tq),
        in_specs=[pl.BlockSpec((1, tq, 2 * B_DH), lambda b, h, i: (b, i, h)),
                  pl.BlockSpec((1, s, 2 * B_DH), lambda b, h, i: (b, 0, h)),
                  pl.BlockSpec((1, s, B_DV), lambda b, h, i: (b, 0, h)),
                  pl.BlockSpec((4, B_DH), lambda b, h, i: (0, 0)),
                  pl.BlockSpec((1, B_DV), lambda b, h, i: (0, 0))],
        out_specs=pl.BlockSpec((1, tq, B_DV), lambda b, h, i: (b, i, h)),
        out_shape=jax.ShapeDtypeStruct((nb, s, B_V), BF16),
        scratch_shapes=[pltpu.VMEM((2, tq, LANES), F32),
                        pltpu.VMEM((2, tq, LANES), F32),
                        pltpu.VMEM((2, tq, B_DV), F32),
                        pltpu.VMEM((2, tq, tk), F32),
                        pltpu.VMEM((2, tq, tk), F32)],
        compiler_params=_cparams(("parallel", "parallel", "arbitrary")),
        name="diff_attn",
    )(q, k, v, lam, norm_g.reshape(1, B_DV))


def kernel(x, c, positions, ada_w, ada_b, norm_g, ffn_w13, ffn_w2, a_w_in, a_conv_w, a_conv_b,
           a_b_if, a_norm_g, a_w_out, b_w_in, b_lam, b_norm_g, b_w_out, final_g):
    depth = ada_w.shape[0]
    d = x.shape[-1]
    ada = _ada_all(c, ada_w, ada_b)
    cos_t, sin_t = _rope_tables(positions)

    nqk = 2 * A_QK
    n_main = nqk + 2 * A_V
    for i in range(depth):
        ada_l = ada[i]
        x = _ffn(x, ada_l, norm_g[i, 0], ffn_w13[i, 0].astype(BF16), ffn_w2[i, 0].astype(BF16),
                 final_g, sub=0, final_norm=False)
        j = i // N_MIXERS
        if i % N_MIXERS == 0:
            w_in = a_w_in[j]
            w_main = w_in[:, :n_main].astype(BF16)
            w_gates = jnp.pad(w_in[:, n_main:], ((0, 0), (0, LANES - 2 * A_HEADS))).astype(BF16)
            b_if = jnp.pad(a_b_if[j], (0, LANES - 2 * A_HEADS)).reshape(1, LANES)
            q, k, v, o, gates = _a_in(x, ada_l, norm_g[i, 1], w_main, w_gates,
                                      a_conv_w[j], a_conv_b[j], b_if)
            y = _a_cell(q, k, v, o, gates, a_norm_g[j])
            w_out = a_w_out[j].astype(BF16)
        else:
            lam_init = 0.8 - 0.6 * math.exp(-0.3 * i)
            q, k, v = _b_in(x, ada_l, norm_g[i, 1], b_w_in[j].astype(BF16), cos_t, sin_t)
            y = _b_attn(q, k, v, b_lam[j], b_norm_g[j], lam_init)
            w_out = b_w_out[j].astype(BF16)
        x = _ffn(x, ada_l, norm_g[i, 2], ffn_w13[i, 1].astype(BF16), ffn_w2[i, 1].astype(BF16),
                 final_g, sub=2, final_norm=(i == depth - 1), mix=(y, w_out))
    return x
```

```python
import functools
import math

import jax
import jax.numpy as jnp
from jax import lax
from jax.experimental import pallas as pl
from jax.experimental.pallas import tpu as pltpu

F32 = jnp.float32
BF16 = jnp.bfloat16

EPS = 1e-6
N_MIXERS = 2
A_HEADS = 4
A_DK = 128
A_DV = 256
A_QK = A_HEADS * A_DK
A_V = A_HEADS * A_DV
A_CONV = 4
B_HEADS = 4
B_DH = 128
B_DV = 2 * B_DH
B_QK = B_HEADS * 2 * B_DH
B_V = B_HEADS * B_DV
ROPE_THETA = 500000.0
ROPE_DIM = B_DH // 4

LANES = 128
SUBLANES = 8
VMEM_LIMIT = 56 * 1024 * 1024
NEG_BIG = -1e30
LOG2E = math.log2(math.e)

FFN_TM = 1024
FFN_NSUB = 4
FFN_CW = 256
PROJ_TM = 512
CELL_L = 256
CELL_NCH = 2
ATT_TQ = 512
ATT_TK = 512
ATT_RG = 64


def _cparams(sem):
    return pltpu.CompilerParams(dimension_semantics=sem, vmem_limit_bytes=VMEM_LIMIT)


def _rms(x, g):
    return (x * lax.rsqrt(jnp.mean(x * x, axis=-1, keepdims=True) + EPS)) * g


def _modulate(x, g, shift, scale):
    return _rms(x, g) * (1.0 + scale) + shift


def _ada_rows(ada_ref, sub):
    return (ada_ref[0, 3 * sub:3 * sub + 1, :],
            ada_ref[0, 3 * sub + 1:3 * sub + 2, :],
            ada_ref[0, 3 * sub + 2:3 * sub + 3, :])


def _ada_kernel(c_ref, w_ref, b_ref, o_ref):
    c = c_ref[...]
    cond = (c * jax.nn.sigmoid(c)).astype(BF16)
    o_ref[0] = jnp.dot(cond, w_ref[0].astype(BF16), preferred_element_type=F32) + b_ref[0]


def _ada_all(c, ada_w, ada_b):
    depth, d, n = ada_w.shape
    nb = c.shape[0]
    rows = SUBLANES * pl.cdiv(nb, SUBLANES)
    cp = jnp.pad(c, ((0, rows - nb), (0, 0)))
    tn = 1024
    out = pl.pallas_call(
        _ada_kernel,
        grid=(depth, n // tn),
        in_specs=[pl.BlockSpec((rows, d), lambda l, j: (0, 0)),
                  pl.BlockSpec((1, d, tn), lambda l, j: (l, 0, j)),
                  pl.BlockSpec((1, 1, tn), lambda l, j: (l, 0, j))],
        out_specs=pl.BlockSpec((1, rows, tn), lambda l, j: (l, 0, j)),
        out_shape=jax.ShapeDtypeStruct((depth, rows, n), F32),
        compiler_params=_cparams(("parallel", "parallel")),
        name="ada",
    )(cp, ada_w, ada_b.reshape(depth, 1, n))
    return out[:, :nb].reshape(depth, nb, 9, d)


def _ffn_kernel(*refs, sub, final_norm, fuse_mix, nsub):
    if fuse_mix:
        x_ref, y_ref, wo_ref, ada_ref, g_ref, w1_ref, w3_ref, w2_ref, fg_ref, o_ref, act_scr = refs
    else:
        x_ref, ada_ref, g_ref, w1_ref, w3_ref, w2_ref, fg_ref, o_ref, act_scr = refs
    tm = x_ref.shape[1]
    dff = w2_ref.shape[0]
    tr = tm // nsub
    shift, scale, gt = _ada_rows(ada_ref, sub)
    for r in range(nsub):
        rows = slice(r * tr, (r + 1) * tr)
        x = x_ref[0, rows, :]
        if fuse_mix:
            _, _, gt_mix = _ada_rows(ada_ref, 1)
            x = x + gt_mix * jnp.dot(y_ref[0, rows, :], wo_ref[...], preferred_element_type=F32)
        o_ref[0, rows, :] = x
        h = _modulate(x, g_ref[...], shift, scale).astype(BF16)
        for c in range(dff // FFN_CW):
            cols = slice(c * FFN_CW, (c + 1) * FFN_CW)
            gate = jnp.dot(h, w1_ref[:, cols], preferred_element_type=F32)
            up = jnp.dot(h, w3_ref[:, cols], preferred_element_type=F32)
            act_scr[rows, cols] = ((gate * jax.nn.sigmoid(gate)) * up).astype(BF16)
        acc = jnp.dot(act_scr[rows, :], w2_ref[...], preferred_element_type=F32)
        y = o_ref[0, rows, :] + (0.5 * gt) * acc
        if final_norm:
            y = _rms(y, fg_ref[...])
        o_ref[0, rows, :] = y


def _ffn(x, ada_l, g, w13, w2, widx, final_g, *, sub, final_norm, mix=None):
    nb, s, d = x.shape
    dff = w2.shape[2]
    li, lj = widx
    tm = min(FFN_TM, s)
    assert dff % FFN_CW == 0 and s % tm == 0
    tok = lambda b, i: (b, i, 0)
    const = lambda b, i: (0, 0)
    once = dict(pipeline_mode=pl.Buffered(1))
    in_specs = [pl.BlockSpec((1, tm, d), tok)]
    args = [x]
    if mix is not None:
        y, w_out = mix
        dv = y.shape[-1]
        in_specs += [pl.BlockSpec((1, tm, dv), tok), pl.BlockSpec((dv, d), const, **once)]
        args += [y, w_out]
    in_specs += [pl.BlockSpec((1, 9, d), lambda b, i: (b, 0, 0)),
                 pl.BlockSpec((1, d), const),
                 pl.BlockSpec((None, None, d, dff), lambda b, i: (li, lj, 0, 0), **once),
                 pl.BlockSpec((None, None, d, dff), lambda b, i: (li, lj, 0, 1), **once),
                 pl.BlockSpec((None, None, dff, d), lambda b, i: (li, lj, 0, 0), **once),
                 pl.BlockSpec((1, d), const)]
    args += [ada_l, g.reshape(1, d), w13, w13, w2, final_g.reshape(1, d)]
    return pl.pallas_call(
        functools.partial(_ffn_kernel, sub=sub, final_norm=final_norm,
                          fuse_mix=mix is not None, nsub=FFN_NSUB),
        grid=(nb, s // tm),
        in_specs=in_specs,
        out_specs=pl.BlockSpec((1, tm, d), tok),
        out_shape=jax.ShapeDtypeStruct(x.shape, F32),
        scratch_shapes=[pltpu.VMEM((tm, dff), BF16)],
        compiler_params=_cparams(("parallel", "arbitrary")),
        name="ffn",
    )(*args)


def _a_in_kernel(x_ref, xh_ref, ada_ref, g_ref, w_ref, wg_ref, cw_ref, cb_ref, bif_ref,
                 q_ref, k_ref, v_ref, o_ref, gates_ref, ext_scr):
    i = pl.program_id(1)
    tm = x_ref.shape[1]
    halo = xh_ref.shape[1]
    shift, scale, _ = _ada_rows(ada_ref, 1)
    g = g_ref[...]
    h = _modulate(x_ref[0], g, shift, scale).astype(BF16)
    hh = _modulate(xh_ref[0], g, shift, scale).astype(BF16)

    nqk = 2 * A_QK
    wqk = w_ref[:, 0:nqk]
    pre = jnp.dot(h, wqk, preferred_element_type=F32)
    pre_h = jnp.dot(hh, wqk, preferred_element_type=F32)
    ext_scr[0:halo, :] = jnp.where(i == 0, 0.0, pre_h)
    ext_scr[halo:halo + tm, :] = pre
    conv = cb_ref[...] + cw_ref[A_CONV - 1:A_CONV, :] * pre
    for j in range(A_CONV - 1):
        off = halo - (A_CONV - 1) + j
        conv = conv + cw_ref[j:j + 1, :] * ext_scr[off:off + tm, :]
    qk = conv * jax.nn.sigmoid(conv)
    q_ref[0] = qk[:, 0:A_QK].astype(BF16)
    k_ref[0] = (qk[:, A_QK:nqk] * (A_DK ** -0.5)).astype(BF16)

    v_ref[0] = jnp.dot(h, w_ref[:, nqk:nqk + A_V], preferred_element_type=F32).astype(BF16)
    o_ref[0] = jnp.dot(h, w_ref[:, nqk + A_V:nqk + 2 * A_V], preferred_element_type=F32)

    gp = jnp.dot(h, wg_ref[...], preferred_element_type=F32) + bif_ref[...]
    lane = lax.broadcasted_iota(jnp.int32, gp.shape, 1)
    gates_ref[0] = jnp.where(lane < A_HEADS, gp, jax.nn.log_sigmoid(gp))


def _a_in(x, ada_l, g, w_main, w_gates, conv_w, conv_b, b_if):
    nb, s, d = x.shape
    tm = min(PROJ_TM, s)
    halo = SUBLANES
    nqk = 2 * A_QK
    hb = tm // halo
    tok = lambda b, i: (b, i, 0)
    return pl.pallas_call(
        _a_in_kernel,
        grid=(nb, s // tm),
        in_specs=[pl.BlockSpec((1, tm, d), tok),
                  pl.BlockSpec((1, halo, d), lambda b, i: (b, jnp.maximum(i * hb - 1, 0), 0)),
                  pl.BlockSpec((1, 9, d), lambda b, i: (b, 0, 0)),
                  pl.BlockSpec((1, d), lambda b, i: (0, 0)),
                  pl.BlockSpec(w_main.shape, lambda b, i: (0, 0)),
                  pl.BlockSpec(w_gates.shape, lambda b, i: (0, 0)),
                  pl.BlockSpec((A_CONV, nqk), lambda b, i: (0, 0)),
                  pl.BlockSpec((1, nqk), lambda b, i: (0, 0)),
                  pl.BlockSpec((1, LANES), lambda b, i: (0, 0))],
        out_specs=[pl.BlockSpec((1, tm, A_QK), tok),
                   pl.BlockSpec((1, tm, A_QK), tok),
                   pl.BlockSpec((1, tm, A_V), tok),
                   pl.BlockSpec((1, tm, A_V), tok),
                   pl.BlockSpec((1, tm, LANES), tok)],
        out_shape=[jax.ShapeDtypeStruct((nb, s, A_QK), BF16),
                   jax.ShapeDtypeStruct((nb, s, A_QK), BF16),
                   jax.ShapeDtypeStruct((nb, s, A_V), BF16),
                   jax.ShapeDtypeStruct((nb, s, A_V), F32),
                   jax.ShapeDtypeStruct((nb, s, LANES), F32)],
        scratch_shapes=[pltpu.VMEM((halo + tm, nqk), F32)],
        compiler_params=_cparams(("parallel", "arbitrary")),
        name="mlstm_in",
    )(x, x, ada_l, g.reshape(1, d), w_main, w_gates, conv_w, conv_b.reshape(1, nqk), b_if)


def _scan_rows(x, op, fill):
    n = x.shape[0]
    row = lax.broadcasted_iota(jnp.int32, x.shape, 0)
    d = 1
    while d < n:
        x = op(x, jnp.where(row >= d, pltpu.roll(x, d, axis=0), fill))
        d *= 2
    return x


def _a_cell_kernel(q_ref, k_ref, v_ref, o_ref, gates_ref, ng_ref, y_ref,
                   c_scr, n_scr, m_scr, *, L):

    @pl.when(pl.program_id(1) == 0)
    def _():
        c_scr[...] = jnp.zeros_like(c_scr)
        n_scr[...] = jnp.zeros_like(n_scr)
        m_scr[...] = jnp.zeros_like(m_scr)

    row = lax.broadcasted_iota(jnp.int32, (L, L), 0)
    col = lax.broadcasted_iota(jnp.int32, (L, L), 1)
    causal = col <= row

    for r in range(q_ref.shape[1] // L):
        rows = slice(r * L, (r + 1) * L)
        gates = gates_ref[0, rows, :]
        log_i = gates
        log_f = pltpu.roll(gates, LANES - A_HEADS, axis=1)
        b = _scan_rows(log_f, jnp.add, 0.0)
        g = log_i - b
        m_prev = m_scr[...]
        big_m = jnp.maximum(m_prev, _scan_rows(g, jnp.maximum, NEG_BIG))
        inter = jnp.exp(m_prev - big_m)
        m_t = b + big_m
        floor = jnp.exp(-m_t)
        m_last = big_m[L - 1:L, :]
        w_col = jnp.exp(g - m_last)
        decay = inter[L - 1:L, :]
        m_scr[...] = m_t[L - 1:L, :]
        g_t = jnp.transpose(g)

        for h in range(A_HEADS):
            qh = q_ref[0, rows, h * A_DK:(h + 1) * A_DK]
            kh = k_ref[0, rows, h * A_DK:(h + 1) * A_DK]
            vh = v_ref[0, rows, h * A_DV:(h + 1) * A_DV]
            dw = jnp.exp(jnp.where(causal, g_t[h:h + 1, :] - big_m[:, h:h + 1], NEG_BIG))
            s = lax.dot_general(qh, kh, (((1,), (1,)), ((), ())),
                                preferred_element_type=F32) * dw
            inter_h = inter[:, h:h + 1]
            c_h = c_scr[h]
            num = inter_h * jnp.dot(qh, c_h.astype(BF16), preferred_element_type=F32) \
                + jnp.dot(s.astype(BF16), vh, preferred_element_type=F32)
            qn = jnp.sum(qh.astype(F32) * n_scr[h], axis=-1, keepdims=True)
            den = inter_h * qn + jnp.sum(s, axis=-1, keepdims=True)
            hv = num / jnp.maximum(jnp.abs(den), floor[:, h:h + 1])
            hv = hv * lax.rsqrt(jnp.mean(hv * hv, axis=-1, keepdims=True) + EPS)
            og = o_ref[0, rows, h * A_DV:(h + 1) * A_DV]
            y = jax.nn.sigmoid(og) * (hv * ng_ref[:, h * A_DV:(h + 1) * A_DV])
            y_ref[0, rows, h * A_DV:(h + 1) * A_DV] = y.astype(BF16)

            kw = kh.astype(F32) * w_col[:, h:h + 1]
            dec = decay[:, h:h + 1]
            c_scr[h] = dec * c_h + jnp.dot(jnp.transpose(kw).astype(BF16), vh,
                                           preferred_element_type=F32)
            n_scr[h] = dec * n_scr[h] + jnp.sum(kw, axis=0, keepdims=True)


def _a_cell(q, k, v, o, gates, norm_g):
    nb, s, _ = q.shape
    chunk = min(CELL_L, s)
    L = min(CELL_L * CELL_NCH, s)
    tok = lambda b, c: (b, c, 0)
    return pl.pallas_call(
        functools.partial(_a_cell_kernel, L=chunk),
        grid=(nb, s // L),
        in_specs=[pl.BlockSpec((1, L, A_QK), tok),
                  pl.BlockSpec((1, L, A_QK), tok),
                  pl.BlockSpec((1, L, A_V), tok),
                  pl.BlockSpec((1, L, A_V), tok),
                  pl.BlockSpec((1, L, LANES), tok),
                  pl.BlockSpec((1, A_V), lambda b, c: (0, 0))],
        out_specs=pl.BlockSpec((1, L, A_V), tok),
        out_shape=jax.ShapeDtypeStruct((nb, s, A_V), BF16),
        scratch_shapes=[pltpu.VMEM((A_HEADS, A_DK, A_DV), F32),
                        pltpu.VMEM((A_HEADS, 1, A_DK), F32),
                        pltpu.VMEM((1, LANES), F32)],
        compiler_params=_cparams(("parallel", "arbitrary")),
        name="mlstm_cell",
    )(q, k, v, o, gates, norm_g.reshape(1, A_V))


def _rope_tab_kernel(pos_ref, invf_ref, cos_ref, sin_ref):
    ang = pos_ref[0].astype(F32) * invf_ref[...]
    lane = lax.broadcasted_iota(jnp.int32, ang.shape, 1)
    half = ROPE_DIM // 2
    sn = jnp.sin(ang)
    cos_ref[0] = jnp.where(lane < ROPE_DIM, jnp.cos(ang), 1.0)
    sin_ref[0] = jnp.where(lane < half, -sn, jnp.where(lane < ROPE_DIM, sn, 0.0))


def _rope_tables(positions):
    nb, s = positions.shape
    half = ROPE_DIM // 2
    inv_freq = ROPE_THETA ** (-jnp.arange(0, ROPE_DIM, 2, dtype=F32) / ROPE_DIM)
    invf = jnp.zeros((1, LANES), F32).at[0, :ROPE_DIM].set(jnp.tile(inv_freq, 2))
    tm = min(1024, s)
    tok = lambda b, i: (b, i, 0)
    return pl.pallas_call(
        _rope_tab_kernel,
        grid=(nb, s // tm),
        in_specs=[pl.BlockSpec((1, tm, 1), tok),
                  pl.BlockSpec((1, LANES), lambda b, i: (0, 0))],
        out_specs=[pl.BlockSpec((1, tm, LANES), tok), pl.BlockSpec((1, tm, LANES), tok)],
        out_shape=[jax.ShapeDtypeStruct((nb, s, LANES), F32)] * 2,
        compiler_params=_cparams(("parallel", "parallel")),
        name="rope_tables",
    )(positions.reshape(nb, s, 1), invf)


def _rope(x, cos_t, sin_t):
    half = ROPE_DIM // 2
    lane = lax.broadcasted_iota(jnp.int32, cos_t.shape, 1)
    outs = []
    for c in range(x.shape[1] // B_DH):
        xg = x[:, c * B_DH:(c + 1) * B_DH]
        swapped = jnp.where(lane < half, pltpu.roll(xg, B_DH - half, axis=1),
                            pltpu.roll(xg, half, axis=1))
        outs.append(xg * cos_t + swapped * sin_t)
    return jnp.concatenate(outs, axis=1)


def _b_in_kernel(x_ref, ada_ref, g_ref, w_ref, cos_ref, sin_ref, q_ref, k_ref, v_ref):
    shift, scale, _ = _ada_rows(ada_ref, 1)
    h = _modulate(x_ref[0], g_ref[...], shift, scale).astype(BF16)
    cos_t = cos_ref[0]
    sin_t = sin_ref[0]
    q = jnp.dot(h, w_ref[:, 0:B_QK], preferred_element_type=F32)
    q_ref[0] = (_rope(q, cos_t, sin_t) * (B_DH ** -0.5 * LOG2E)).astype(BF16)
    k = jnp.dot(h, w_ref[:, B_QK:2 * B_QK], preferred_element_type=F32)
    k_ref[0] = _rope(k, cos_t, sin_t).astype(BF16)
    v_ref[0] = jnp.dot(h, w_ref[:, 2 * B_QK:2 * B_QK + B_V],
                       preferred_element_type=F32).astype(BF16)


def _b_in(x, ada_l, g, w_in, cos_t, sin_t):
    nb, s, d = x.shape
    tm = min(PROJ_TM, s)
    tok = lambda b, i: (b, i, 0)
    return pl.pallas_call(
        _b_in_kernel,
        grid=(nb, s // tm),
        in_specs=[pl.BlockSpec((1, tm, d), tok),
                  pl.BlockSpec((1, 9, d), lambda b, i: (b, 0, 0)),
                  pl.BlockSpec((1, d), lambda b, i: (0, 0)),
                  pl.BlockSpec(w_in.shape, lambda b, i: (0, 0)),
                  pl.BlockSpec((1, tm, LANES), tok),
                  pl.BlockSpec((1, tm, LANES), tok)],
        out_specs=[pl.BlockSpec((1, tm, B_QK), tok),
                   pl.BlockSpec((1, tm, B_QK), tok),
                   pl.BlockSpec((1, tm, B_V), tok)],
        out_shape=[jax.ShapeDtypeStruct((nb, s, B_QK), BF16),
                   jax.ShapeDtypeStruct((nb, s, B_QK), BF16),
                   jax.ShapeDtypeStruct((nb, s, B_V), BF16)],
        compiler_params=_cparams(("parallel", "parallel")),
        name="attn_in",
    )(x, ada_l, g.reshape(1, d), w_in, cos_t, sin_t)


def _b_attn_kernel(q_ref, k_ref, v_ref, lam_ref, ng_ref, y_ref, m_scr, l_scr, acc_scr,
                   sa_scr, sb_scr, p_scr, a_scr, *, lam_init, tk):
    qi = pl.program_id(2)
    tq = q_ref.shape[1]
    nchunk = tk // LANES

    m_scr[...] = jnp.full_like(m_scr, NEG_BIG)
    l_scr[...] = jnp.zeros_like(l_scr)
    acc_scr[...] = jnp.zeros_like(acc_scr)

    def scores(j, dst):
        ks = pl.multiple_of(j * tk, tk)
        for c in range(2):
            qc = q_ref[0, :, c * B_DH:(c + 1) * B_DH]
            kc = k_ref[0, pl.ds(ks, tk), c * B_DH:(c + 1) * B_DH]
            dst[c] = lax.dot_general(qc, kc, (((1,), (1,)), ((), ())),
                                     preferred_element_type=F32)

    def update(j, src, masked):
        ks = pl.multiple_of(j * tk, tk)
        v = v_ref[0, pl.ds(ks, tk), :]
        for c in range(2):
            for rg in range(tq // ATT_RG):
                rows = slice(rg * ATT_RG, (rg + 1) * ATT_RG)
                s = src[c, rows, :]
                if masked:
                    row = lax.broadcasted_iota(jnp.int32, (ATT_RG, tk), 0) + rg * ATT_RG
                    col = lax.broadcasted_iota(jnp.int32, (ATT_RG, tk), 1)
                    s = jnp.where(col <= row, s, NEG_BIG)
                m_old = m_scr[c, rows, :]
                m_new = jnp.maximum(m_old, jnp.max(s, axis=-1, keepdims=True))
                alpha = jnp.exp2(m_old - m_new)
                chunks = [jnp.exp2(s[:, n * LANES:(n + 1) * LANES] - m_new)
                          for n in range(nchunk)]
                part = chunks[0]
                for n in range(1, nchunk):
                    part = part + chunks[n]
                l_scr[c, rows, :] = alpha * l_scr[c, rows, :] + part
                m_scr[c, rows, :] = m_new
                a_scr[c, rows, :] = alpha
                p_scr[c, rows, :] = jnp.concatenate(chunks, axis=1).astype(BF16)
            alpha = a_scr[c]
            alpha_v = jnp.concatenate([alpha] * (B_DV // LANES), axis=1)
            acc_scr[c] = alpha_v * acc_scr[c] + jnp.dot(p_scr[c], v, preferred_element_type=F32)

    scores(0, sa_scr)

    def body(i, carry):
        j = 2 * i
        scores(j + 1, sb_scr)
        update(j, sa_scr, False)
        scores(j + 2, sa_scr)
        update(j + 1, sb_scr, False)
        return carry

    lax.fori_loop(0, qi // 2, body, 0)

    @pl.when(qi % 2 == 1)
    def _():
        scores(qi, sb_scr)
        update(qi - 1, sa_scr, False)
        update(qi, sb_scr, True)

    @pl.when(qi % 2 == 0)
    def _():
        update(qi, sa_scr, True)

    lf = lam_ref[...]
    lam_full = (jnp.exp(jnp.sum(lf[0:1] * lf[1:2], axis=-1, keepdims=True))
                - jnp.exp(jnp.sum(lf[2:3] * lf[3:4], axis=-1, keepdims=True)) + lam_init)
    l0 = jnp.sum(l_scr[0], axis=-1, keepdims=True)
    l1 = jnp.sum(l_scr[1], axis=-1, keepdims=True)
    o = acc_scr[0] / l0 - lam_full * (acc_scr[1] / l1)
    o = o * lax.rsqrt(jnp.mean(o * o, axis=-1, keepdims=True) + EPS)
    y_ref[0] = ((o * ng_ref[...]) * (1.0 - lam_init)).astype(BF16)


def _b_attn(q, k, v, lam, norm_g, lam_init):
    nb, s, _ = q.shape
    tq = min(ATT_TQ, s)
    tk = tq
    return pl.pallas_call(
        functools.partial(_b_attn_kernel, lam_init=lam_init, tk=tk),
        grid=(nb, B_HEADS, s // tq),
        in_specs=[pl.BlockSpec((1, tq, 2 * B_DH), lambda b, h, i: (b, i, h)),
                  pl.BlockSpec((1, s, 2 * B_DH), lambda b, h, i: (b, 0, h)),
                  pl.BlockSpec((1, s, B_DV), lambda b, h, i: (b, 0, h)),
                  pl.BlockSpec((4, B_DH), lambda b, h, i: (0, 0)),
                  pl.BlockSpec((1, B_DV), lambda b, h, i: (0, 0))],
        out_specs=pl.BlockSpec((1, tq, B_DV), lambda b, h, i: (b, i, h)),
        out_shape=jax.ShapeDtypeStruct((nb, s, B_V), BF16),
        scratch_shapes=[pltpu.VMEM((2, tq, LANES), F32),
                        pltpu.VMEM((2, tq, LANES), F32),
                        pltpu.VMEM((2, tq, B_DV), F32),
                        pltpu.VMEM((2, tq, tk), F32),
                        pltpu.VMEM((2, tq, tk), F32),
                        pltpu.VMEM((2, tq, tk), BF16),
                        pltpu.VMEM((2, tq, LANES), F32)],
        compiler_params=_cparams(("parallel", "parallel", "arbitrary")),
        name="diff_attn",
    )(q, k, v, lam, norm_g.reshape(1, B_DV))


def kernel(x, c, positions, ada_w, ada_b, norm_g, ffn_w13, ffn_w2, a_w_in, a_conv_w, a_conv_b,
           a_b_if, a_norm_g, a_w_out, b_w_in, b_lam, b_norm_g, b_w_out, final_g):
    depth = ada_w.shape[0]
    d = x.shape[-1]
    ada = _ada_all(c, ada_w, ada_b)
    cos_t, sin_t = _rope_tables(positions)

    nqk = 2 * A_QK
    n_main = nqk + 2 * A_V
    w13_bf = ffn_w13.astype(BF16)
    w2_bf = ffn_w2.astype(BF16)
    for i in range(depth):
        ada_l = ada[i]
        x = _ffn(x, ada_l, norm_g[i, 0], w13_bf, w2_bf, (i, 0), final_g, sub=0, final_norm=False)
        j = i // N_MIXERS
        if i % N_MIXERS == 0:
            w_in = a_w_in[j]
            w_main = w_in[:, :n_main].astype(BF16)
            w_gates = jnp.pad(w_in[:, n_main:], ((0, 0), (0, LANES - 2 * A_HEADS))).astype(BF16)
            b_if = jnp.pad(a_b_if[j], (0, LANES - 2 * A_HEADS)).reshape(1, LANES)
            q, k, v, o, gates = _a_in(x, ada_l, norm_g[i, 1], w_main, w_gates,
                                      a_conv_w[j], a_conv_b[j], b_if)
            y = _a_cell(q, k, v, o, gates, a_norm_g[j])
            w_out = a_w_out[j].astype(BF16)
        else:
            lam_init = 0.8 - 0.6 * math.exp(-0.3 * i)
            q, k, v = _b_in(x, ada_l, norm_g[i, 1], b_w_in[j].astype(BF16), cos_t, sin_t)
            y = _b_attn(q, k, v, b_lam[j], b_norm_g[j], lam_init)
            w_out = b_w_out[j].astype(BF16)
        x = _ffn(x, ada_l, norm_g[i, 2], w13_bf, w2_bf, (i, 1), final_g, sub=2,
                 final_norm=(i == depth - 1), mix=(y, w_out))
    return x
```

```python
import functools
import math

import jax
import jax.numpy as jnp
from jax import lax
from jax.experimental import pallas as pl
from jax.experimental.pallas import tpu as pltpu

F32 = jnp.float32
BF16 = jnp.bfloat16

EPS = 1e-6
N_MIXERS = 2
A_HEADS = 4
A_DK = 128
A_DV = 256
A_QK = A_HEADS * A_DK
A_V = A_HEADS * A_DV
A_CONV = 4
B_HEADS = 4
B_DH = 128
B_DV = 2 * B_DH
B_QK = B_HEADS * 2 * B_DH
B_V = B_HEADS * B_DV
ROPE_THETA = 500000.0
ROPE_DIM = B_DH // 4

LANES = 128
SUBLANES = 8
VMEM_LIMIT = 56 * 1024 * 1024
NEG_BIG = -1e30
LOG2E = math.log2(math.e)

FFN_TM = 1024
FFN_NSUB = 4
FFN_CW = 256
PROJ_TM = 512
PROJ_NSUB = 2
CELL_L = 256
CELL_NCH = 2
ATT_TQ = 512
ATT_TK = 512


def _cparams(sem):
    return pltpu.CompilerParams(dimension_semantics=sem, vmem_limit_bytes=VMEM_LIMIT)


def _rms(x, g):
    return (x * lax.rsqrt(jnp.mean(x * x, axis=-1, keepdims=True) + EPS)) * g


def _modulate(x, g, shift, scale):
    return _rms(x, g) * (1.0 + scale) + shift


def _ada_rows(ada_ref, sub):
    return (ada_ref[0, 3 * sub:3 * sub + 1, :],
            ada_ref[0, 3 * sub + 1:3 * sub + 2, :],
            ada_ref[0, 3 * sub + 2:3 * sub + 3, :])


def _ada_kernel(c_ref, w_ref, b_ref, o_ref):
    c = c_ref[...]
    cond = (c * jax.nn.sigmoid(c)).astype(BF16)
    o_ref[0] = jnp.dot(cond, w_ref[0].astype(BF16), preferred_element_type=F32) + b_ref[0]


def _ada_all(c, ada_w, ada_b):
    depth, d, n = ada_w.shape
    nb = c.shape[0]
    rows = SUBLANES * pl.cdiv(nb, SUBLANES)
    cp = jnp.pad(c, ((0, rows - nb), (0, 0)))
    tn = 1024
    out = pl.pallas_call(
        _ada_kernel,
        grid=(depth, n // tn),
        in_specs=[pl.BlockSpec((rows, d), lambda l, j: (0, 0)),
                  pl.BlockSpec((1, d, tn), lambda l, j: (l, 0, j)),
                  pl.BlockSpec((1, 1, tn), lambda l, j: (l, 0, j))],
        out_specs=pl.BlockSpec((1, rows, tn), lambda l, j: (l, 0, j)),
        out_shape=jax.ShapeDtypeStruct((depth, rows, n), F32),
        compiler_params=_cparams(("parallel", "parallel")),
        name="ada",
    )(cp, ada_w, ada_b.reshape(depth, 1, n))
    return out[:, :nb].reshape(depth, nb, 9, d)


def _ffn_kernel(*refs, sub, final_norm, fuse_mix, nsub):
    if fuse_mix:
        x_ref, y_ref, wo_ref, ada_ref, g_ref, w1_ref, w3_ref, w2_ref, fg_ref, o_ref, act_scr = refs
    else:
        x_ref, ada_ref, g_ref, w1_ref, w3_ref, w2_ref, fg_ref, o_ref, act_scr = refs
    tm = x_ref.shape[1]
    dff = w2_ref.shape[0]
    tr = tm // nsub
    shift, scale, gt = _ada_rows(ada_ref, sub)
    for r in range(nsub):
        rows = slice(r * tr, (r + 1) * tr)
        x = x_ref[0, rows, :]
        if fuse_mix:
            _, _, gt_mix = _ada_rows(ada_ref, 1)
            x = x + gt_mix * jnp.dot(y_ref[0, rows, :], wo_ref[...], preferred_element_type=F32)
        o_ref[0, rows, :] = x
        h = _modulate(x, g_ref[...], shift, scale).astype(BF16)
        for c in range(dff // FFN_CW):
            cols = slice(c * FFN_CW, (c + 1) * FFN_CW)
            gate = jnp.dot(h, w1_ref[:, cols], preferred_element_type=F32)
            up = jnp.dot(h, w3_ref[:, cols], preferred_element_type=F32)
            act_scr[rows, cols] = ((gate * jax.nn.sigmoid(gate)) * up).astype(BF16)
        acc = jnp.dot(act_scr[rows, :], w2_ref[...], preferred_element_type=F32)
        y = o_ref[0, rows, :] + (0.5 * gt) * acc
        if final_norm:
            y = _rms(y, fg_ref[...])
        o_ref[0, rows, :] = y


def _ffn(x, ada_l, g, w13, w2, widx, final_g, *, sub, final_norm, mix=None):
    nb, s, d = x.shape
    dff = w2.shape[2]
    li, lj = widx
    tm = min(FFN_TM, s)
    assert dff % FFN_CW == 0 and s % tm == 0
    tok = lambda b, i: (b, i, 0)
    const = lambda b, i: (0, 0)
    once = dict(pipeline_mode=pl.Buffered(1))
    in_specs = [pl.BlockSpec((1, tm, d), tok)]
    args = [x]
    if mix is not None:
        y, w_out = mix
        dv = y.shape[-1]
        in_specs += [pl.BlockSpec((1, tm, dv), tok), pl.BlockSpec((dv, d), const, **once)]
        args += [y, w_out]
    in_specs += [pl.BlockSpec((1, 9, d), lambda b, i: (b, 0, 0)),
                 pl.BlockSpec((1, d), const),
                 pl.BlockSpec((None, None, d, dff), lambda b, i: (li, lj, 0, 0), **once),
                 pl.BlockSpec((None, None, d, dff), lambda b, i: (li, lj, 0, 1), **once),
                 pl.BlockSpec((None, None, dff, d), lambda b, i: (li, lj, 0, 0), **once),
                 pl.BlockSpec((1, d), const)]
    args += [ada_l, g.reshape(1, d), w13, w13, w2, final_g.reshape(1, d)]
    return pl.pallas_call(
        functools.partial(_ffn_kernel, sub=sub, final_norm=final_norm,
                          fuse_mix=mix is not None, nsub=FFN_NSUB),
        grid=(nb, s // tm),
        in_specs=in_specs,
        out_specs=pl.BlockSpec((1, tm, d), tok),
        out_shape=jax.ShapeDtypeStruct(x.shape, F32),
        scratch_shapes=[pltpu.VMEM((tm, dff), BF16)],
        compiler_params=_cparams(("parallel", "arbitrary")),
        name="ffn",
    )(*args)


def _a_in_kernel(x_ref, xh_ref, ada_ref, g_ref, w_ref, wg_ref, cw_ref, cb_ref, bif_ref,
                 q_ref, k_ref, v_ref, o_ref, gates_ref, ext_scr):
    i = pl.program_id(1)
    tm = x_ref.shape[1]
    halo = xh_ref.shape[1]
    shift, scale, _ = _ada_rows(ada_ref, 1)
    g = g_ref[...]
    nqk = 2 * A_QK
    wqk = w_ref[:, 0:nqk]
    hh = _modulate(xh_ref[0], g, shift, scale).astype(BF16)
    pre_h = jnp.dot(hh, wqk, preferred_element_type=F32)
    ext_scr[0:halo, :] = jnp.where(i == 0, 0.0, pre_h)

    tr = tm // PROJ_NSUB
    for r in range(PROJ_NSUB):
        rows = slice(r * tr, (r + 1) * tr)
        h = _modulate(x_ref[0, rows, :], g, shift, scale).astype(BF16)
        pre = jnp.dot(h, wqk, preferred_element_type=F32)
        ext_scr[halo + r * tr:halo + (r + 1) * tr, :] = pre
        conv = cb_ref[...] + cw_ref[A_CONV - 1:A_CONV, :] * pre
        for j in range(A_CONV - 1):
            off = halo - (A_CONV - 1) + j + r * tr
            conv = conv + cw_ref[j:j + 1, :] * ext_scr[off:off + tr, :]
        qk = conv * jax.nn.sigmoid(conv)
        q_ref[0, rows, :] = qk[:, 0:A_QK].astype(BF16)
        k_ref[0, rows, :] = (qk[:, A_QK:nqk] * (A_DK ** -0.5)).astype(BF16)

        v_ref[0, rows, :] = jnp.dot(h, w_ref[:, nqk:nqk + A_V],
                                    preferred_element_type=F32).astype(BF16)
        o_ref[0, rows, :] = jnp.dot(h, w_ref[:, nqk + A_V:nqk + 2 * A_V],
                                    preferred_element_type=F32)

        gp = jnp.dot(h, wg_ref[...], preferred_element_type=F32) + bif_ref[...]
        lane = lax.broadcasted_iota(jnp.int32, gp.shape, 1)
        gates_ref[0, rows, :] = jnp.where(lane < A_HEADS, gp, jax.nn.log_sigmoid(gp))


def _a_in(x, ada_l, g, w_main, w_gates, conv_w, conv_b, b_if):
    nb, s, d = x.shape
    tm = min(PROJ_TM, s)
    halo = SUBLANES
    nqk = 2 * A_QK
    hb = tm // halo
    tok = lambda b, i: (b, i, 0)
    return pl.pallas_call(
        _a_in_kernel,
        grid=(nb, s // tm),
        in_specs=[pl.BlockSpec((1, tm, d), tok),
                  pl.BlockSpec((1, halo, d), lambda b, i: (b, jnp.maximum(i * hb - 1, 0), 0)),
                  pl.BlockSpec((1, 9, d), lambda b, i: (b, 0, 0)),
                  pl.BlockSpec((1, d), lambda b, i: (0, 0)),
                  pl.BlockSpec(w_main.shape, lambda b, i: (0, 0)),
                  pl.BlockSpec(w_gates.shape, lambda b, i: (0, 0)),
                  pl.BlockSpec((A_CONV, nqk), lambda b, i: (0, 0)),
                  pl.BlockSpec((1, nqk), lambda b, i: (0, 0)),
                  pl.BlockSpec((1, LANES), lambda b, i: (0, 0))],
        out_specs=[pl.BlockSpec((1, tm, A_QK), tok),
                   pl.BlockSpec((1, tm, A_QK), tok),
                   pl.BlockSpec((1, tm, A_V), tok),
                   pl.BlockSpec((1, tm, A_V), tok),
                   pl.BlockSpec((1, tm, LANES), tok)],
        out_shape=[jax.ShapeDtypeStruct((nb, s, A_QK), BF16),
                   jax.ShapeDtypeStruct((nb, s, A_QK), BF16),
                   jax.ShapeDtypeStruct((nb, s, A_V), BF16),
                   jax.ShapeDtypeStruct((nb, s, A_V), F32),
                   jax.ShapeDtypeStruct((nb, s, LANES), F32)],
        scratch_shapes=[pltpu.VMEM((halo + tm, nqk), F32)],
        compiler_params=_cparams(("parallel", "arbitrary")),
        name="mlstm_in",
    )(x, x, ada_l, g.reshape(1, d), w_main, w_gates, conv_w, conv_b.reshape(1, nqk), b_if)


def _scan_rows(x, op, fill):
    n = x.shape[0]
    row = lax.broadcasted_iota(jnp.int32, x.shape, 0)
    d = 1
    while d < n:
        x = op(x, jnp.where(row >= d, pltpu.roll(x, d, axis=0), fill))
        d *= 2
    return x


def _a_cell_kernel(q_ref, k_ref, v_ref, o_ref, gates_ref, ng_ref, y_ref,
                   c_scr, n_scr, m_scr, *, L):

    nbk = q_ref.shape[0]

    @pl.when(pl.program_id(0) == 0)
    def _():
        c_scr[...] = jnp.zeros_like(c_scr)
        n_scr[...] = jnp.zeros_like(n_scr)
        m_scr[...] = jnp.zeros_like(m_scr)

    row = lax.broadcasted_iota(jnp.int32, (L, L), 0)
    col = lax.broadcasted_iota(jnp.int32, (L, L), 1)
    causal = col <= row

    def gate_terms(bi, rows):
        gates = gates_ref[bi, rows, :]
        log_f = pltpu.roll(gates, LANES - A_HEADS, axis=1)
        b = _scan_rows(log_f, jnp.add, 0.0)
        g = gates - b
        m_prev = m_scr[bi]
        big_m = jnp.maximum(m_prev, _scan_rows(g, jnp.maximum, NEG_BIG))
        inter = jnp.exp(m_prev - big_m)
        m_t = b + big_m
        m_scr[bi] = m_t[L - 1:L, :]
        return dict(big_m=big_m, inter=inter, floor=jnp.exp(-m_t),
                    w_col=jnp.exp(g - big_m[L - 1:L, :]), decay=inter[L - 1:L, :],
                    g_t=jnp.transpose(g))

    def head(bi, h, rows, t):
        qh = q_ref[bi, rows, h * A_DK:(h + 1) * A_DK]
        kh = k_ref[bi, rows, h * A_DK:(h + 1) * A_DK]
        vh = v_ref[bi, rows, h * A_DV:(h + 1) * A_DV]
        st = bi * A_HEADS + h
        dw = jnp.exp(jnp.where(causal, t["g_t"][h:h + 1, :] - t["big_m"][:, h:h + 1], NEG_BIG))
        s = lax.dot_general(qh, kh, (((1,), (1,)), ((), ())), preferred_element_type=F32) * dw
        inter_h = t["inter"][:, h:h + 1]
        c_h = c_scr[st]
        num = inter_h * jnp.dot(qh, c_h.astype(BF16), preferred_element_type=F32) \
            + jnp.dot(s.astype(BF16), vh, preferred_element_type=F32)
        qn = jnp.sum(qh.astype(F32) * n_scr[st], axis=-1, keepdims=True)
        den = inter_h * qn + jnp.sum(s, axis=-1, keepdims=True)
        hv = num / jnp.maximum(jnp.abs(den), t["floor"][:, h:h + 1])
        hv = hv * lax.rsqrt(jnp.mean(hv * hv, axis=-1, keepdims=True) + EPS)
        og = o_ref[bi, rows, h * A_DV:(h + 1) * A_DV]
        y = jax.nn.sigmoid(og) * (hv * ng_ref[:, h * A_DV:(h + 1) * A_DV])
        y_ref[bi, rows, h * A_DV:(h + 1) * A_DV] = y.astype(BF16)

        kw = kh.astype(F32) * t["w_col"][:, h:h + 1]
        dec = t["decay"][:, h:h + 1]
        c_scr[st] = dec * c_h + jnp.dot(jnp.transpose(kw).astype(BF16), vh,
                                        preferred_element_type=F32)
        n_scr[st] = dec * n_scr[st] + jnp.sum(kw, axis=0, keepdims=True)

    for r in range(q_ref.shape[1] // L):
        rows = slice(r * L, (r + 1) * L)
        terms = [gate_terms(bi, rows) for bi in range(nbk)]
        for h in range(A_HEADS):
            for bi in range(nbk):
                head(bi, h, rows, terms[bi])


def _a_cell(q, k, v, o, gates, norm_g):
    nb, s, _ = q.shape
    chunk = min(CELL_L, s)
    L = min(CELL_L * CELL_NCH, s)
    tok = lambda c: (0, c, 0)
    return pl.pallas_call(
        functools.partial(_a_cell_kernel, L=chunk),
        grid=(s // L,),
        in_specs=[pl.BlockSpec((nb, L, A_QK), tok),
                  pl.BlockSpec((nb, L, A_QK), tok),
                  pl.BlockSpec((nb, L, A_V), tok),
                  pl.BlockSpec((nb, L, A_V), tok),
                  pl.BlockSpec((nb, L, LANES), tok),
                  pl.BlockSpec((1, A_V), lambda c: (0, 0))],
        out_specs=pl.BlockSpec((nb, L, A_V), tok),
        out_shape=jax.ShapeDtypeStruct((nb, s, A_V), BF16),
        scratch_shapes=[pltpu.VMEM((nb * A_HEADS, A_DK, A_DV), F32),
                        pltpu.VMEM((nb * A_HEADS, 1, A_DK), F32),
                        pltpu.VMEM((nb, 1, LANES), F32)],
        compiler_params=_cparams(("arbitrary",)),
        name="mlstm_cell",
    )(q, k, v, o, gates, norm_g.reshape(1, A_V))


def _rope_tab_kernel(pos_ref, invf_ref, cos_ref, sin_ref):
    ang = pos_ref[0].astype(F32) * invf_ref[...]
    lane = lax.broadcasted_iota(jnp.int32, ang.shape, 1)
    half = ROPE_DIM // 2
    sn = jnp.sin(ang)
    cos_ref[0] = jnp.where(lane < ROPE_DIM, jnp.cos(ang), 1.0)
    sin_ref[0] = jnp.where(lane < half, -sn, jnp.where(lane < ROPE_DIM, sn, 0.0))


def _rope_tables(positions):
    nb, s = positions.shape
    half = ROPE_DIM // 2
    inv_freq = ROPE_THETA ** (-jnp.arange(0, ROPE_DIM, 2, dtype=F32) / ROPE_DIM)
    invf = jnp.zeros((1, LANES), F32).at[0, :ROPE_DIM].set(jnp.tile(inv_freq, 2))
    tm = min(1024, s)
    tok = lambda b, i: (b, i, 0)
    return pl.pallas_call(
        _rope_tab_kernel,
        grid=(nb, s // tm),
        in_specs=[pl.BlockSpec((1, tm, 1), tok),
                  pl.BlockSpec((1, LANES), lambda b, i: (0, 0))],
        out_specs=[pl.BlockSpec((1, tm, LANES), tok), pl.BlockSpec((1, tm, LANES), tok)],
        out_shape=[jax.ShapeDtypeStruct((nb, s, LANES), F32)] * 2,
        compiler_params=_cparams(("parallel", "parallel")),
        name="rope_tables",
    )(positions.reshape(nb, s, 1), invf)


def _rope(x, cos_t, sin_t):
    half = ROPE_DIM // 2
    lane = lax.broadcasted_iota(jnp.int32, cos_t.shape, 1)
    outs = []
    for c in range(x.shape[1] // B_DH):
        xg = x[:, c * B_DH:(c + 1) * B_DH]
        swapped = jnp.where(lane < half, pltpu.roll(xg, B_DH - half, axis=1),
                            pltpu.roll(xg, half, axis=1))
        outs.append(xg * cos_t + swapped * sin_t)
    return jnp.concatenate(outs, axis=1)


def _b_in_kernel(x_ref, ada_ref, g_ref, w_ref, cos_ref, sin_ref, q_ref, k_ref, v_ref):
    shift, scale, _ = _ada_rows(ada_ref, 1)
    h = _modulate(x_ref[0], g_ref[...], shift, scale).astype(BF16)
    cos_t = cos_ref[0]
    sin_t = sin_ref[0]
    q = jnp.dot(h, w_ref[:, 0:B_QK], preferred_element_type=F32)
    q_ref[0] = (_rope(q, cos_t, sin_t) * (B_DH ** -0.5 * LOG2E)).astype(BF16)
    k = jnp.dot(h, w_ref[:, B_QK:2 * B_QK], preferred_element_type=F32)
    k_ref[0] = _rope(k, cos_t, sin_t).astype(BF16)
    v_ref[0] = jnp.dot(h, w_ref[:, 2 * B_QK:2 * B_QK + B_V],
                       preferred_element_type=F32).astype(BF16)


def _b_in(x, ada_l, g, w_in, cos_t, sin_t):
    nb, s, d = x.shape
    tm = min(PROJ_TM, s)
    tok = lambda b, i: (b, i, 0)
    return pl.pallas_call(
        _b_in_kernel,
        grid=(nb, s // tm),
        in_specs=[pl.BlockSpec((1, tm, d), tok),
                  pl.BlockSpec((1, 9, d), lambda b, i: (b, 0, 0)),
                  pl.BlockSpec((1, d), lambda b, i: (0, 0)),
                  pl.BlockSpec(w_in.shape, lambda b, i: (0, 0)),
                  pl.BlockSpec((1, tm, LANES), tok),
                  pl.BlockSpec((1, tm, LANES), tok)],
        out_specs=[pl.BlockSpec((1, tm, B_QK), tok),
                   pl.BlockSpec((1, tm, B_QK), tok),
                   pl.BlockSpec((1, tm, B_V), tok)],
        out_shape=[jax.ShapeDtypeStruct((nb, s, B_QK), BF16),
                   jax.ShapeDtypeStruct((nb, s, B_QK), BF16),
                   jax.ShapeDtypeStruct((nb, s, B_V), BF16)],
        compiler_params=_cparams(("parallel", "parallel")),
        name="attn_in",
    )(x, ada_l, g.reshape(1, d), w_in, cos_t, sin_t)


def _b_attn_kernel(q_ref, k_ref, v_ref, lam_ref, ng_ref, y_ref, m_scr, l_scr, acc_scr,
                   sa_scr, sb_scr, *, lam_init, tk):
    qi = pl.program_id(2)
    tq = q_ref.shape[1]
    nchunk = tk // LANES

    m_scr[...] = jnp.full_like(m_scr, NEG_BIG)
    l_scr[...] = jnp.zeros_like(l_scr)
    acc_scr[...] = jnp.zeros_like(acc_scr)

    def scores(j, dst):
        ks = pl.multiple_of(j * tk, tk)
        for c in range(2):
            qc = q_ref[0, :, c * B_DH:(c + 1) * B_DH]
            kc = k_ref[0, pl.ds(ks, tk), c * B_DH:(c + 1) * B_DH]
            dst[c] = lax.dot_general(qc, kc, (((1,), (1,)), ((), ())),
                                     preferred_element_type=F32)

    def update(j, src, masked):
        ks = pl.multiple_of(j * tk, tk)
        v = v_ref[0, pl.ds(ks, tk), :]
        if masked:
            row = lax.broadcasted_iota(jnp.int32, (tq, tk), 0)
            col = lax.broadcasted_iota(jnp.int32, (tq, tk), 1)
            keep = col <= row
        for c in range(2):
            s = src[c]
            if masked:
                s = jnp.where(keep, s, NEG_BIG)
            m_old = m_scr[c]
            m_new = jnp.maximum(m_old, jnp.max(s, axis=-1, keepdims=True))
            alpha = jnp.exp2(m_old - m_new)
            chunks = [jnp.exp2(s[:, n * LANES:(n + 1) * LANES] - m_new) for n in range(nchunk)]
            part = chunks[0]
            for n in range(1, nchunk):
                part = part + chunks[n]
            l_scr[c] = alpha * l_scr[c] + part
            pr = jnp.concatenate(chunks, axis=1).astype(BF16)
            alpha_v = jnp.concatenate([alpha] * (B_DV // LANES), axis=1)
            acc_scr[c] = alpha_v * acc_scr[c] + jnp.dot(pr, v, preferred_element_type=F32)
            m_scr[c] = m_new

    buf_a = sa_scr
    buf_b = sb_scr
    scores(0, buf_a)

    def body(i, carry):
        j = 2 * i
        scores(j + 1, buf_b)
        update(j, buf_a, False)
        scores(j + 2, buf_a)
        update(j + 1, buf_b, False)
        return carry

    lax.fori_loop(0, qi // 2, body, 0)

    @pl.when(qi % 2 == 1)
    def _():
        scores(qi, buf_b)
        update(qi - 1, buf_a, False)
        update(qi, buf_b, True)

    @pl.when(qi % 2 == 0)
    def _():
        update(qi, buf_a, True)

    lf = lam_ref[...]
    lam_full = (jnp.exp(jnp.sum(lf[0:1] * lf[1:2], axis=-1, keepdims=True))
                - jnp.exp(jnp.sum(lf[2:3] * lf[3:4], axis=-1, keepdims=True)) + lam_init)
    l0 = jnp.sum(l_scr[0], axis=-1, keepdims=True)
    l1 = jnp.sum(l_scr[1], axis=-1, keepdims=True)
    o = acc_scr[0] / l0 - lam_full * (acc_scr[1] / l1)
    o = o * lax.rsqrt(jnp.mean(o * o, axis=-1, keepdims=True) + EPS)
    y_ref[0] = ((o * ng_ref[...]) * (1.0 - lam_init)).astype(BF16)


def _b_attn(q, k, v, lam, norm_g, lam_init):
    nb, s, _ = q.shape
    tq = min(ATT_TQ, s)
    tk = tq
    return pl.pallas_call(
        functools.partial(_b_attn_kernel, lam_init=lam_init, tk=tk),
        grid=(nb, B_HEADS, s // tq),
        in_specs=[pl.BlockSpec((1, tq, 2 * B_DH), lambda b, h, i: (b, i, h)),
                  pl.BlockSpec((1, s, 2 * B_DH), lambda b, h, i: (b, 0, h)),
                  pl.BlockSpec((1, s, B_DV), lambda b, h, i: (b, 0, h)),
                  pl.BlockSpec((4, B_DH), lambda b, h, i: (0, 0)),
                  pl.BlockSpec((1, B_DV), lambda b, h, i: (0, 0))],
        out_specs=pl.BlockSpec((1, tq, B_DV), lambda b, h, i: (b, i, h)),
        out_shape=jax.ShapeDtypeStruct((nb, s, B_V), BF16),
        scratch_shapes=[pltpu.VMEM((2, tq, LANES), F32),
                        pltpu.VMEM((2, tq, LANES), F32),
                        pltpu.VMEM((2, tq, B_DV), F32),
                        pltpu.VMEM((2, tq, tk), F32),
                        pltpu.VMEM((2, tq, tk), F32)],
        compiler_params=_cparams(("parallel", "parallel", "arbitrary")),
        name="diff_attn",
    )(q, k, v, lam, norm_g.reshape(1, B_DV))


def kernel(x, c, positions, ada_w, ada_b, norm_g, ffn_w13, ffn_w2, a_w_in, a_conv_w, a_conv_b,
           a_b_if, a_norm_g, a_w_out, b_w_in, b_lam, b_norm_g, b_w_out, final_g):
    depth = ada_w.shape[0]
    d = x.shape[-1]
    ada = _ada_all(c, ada_w, ada_b)
    cos_t, sin_t = _rope_tables(positions)

    nqk = 2 * A_QK
    n_main = nqk + 2 * A_V
    w13_bf = ffn_w13.astype(BF16)
    w2_bf = ffn_w2.astype(BF16)
    for i in range(depth):
        ada_l = ada[i]
        x = _ffn(x, ada_l, norm_g[i, 0], w13_bf, w2_bf, (i, 0), final_g, sub=0, final_norm=False)
        j = i // N_MIXERS
        if i % N_MIXERS == 0:
            w_in = a_w_in[j]
            w_main = w_in[:, :n_main].astype(BF16)
            w_gates = jnp.pad(w_in[:, n_main:], ((0, 0), (0, LANES - 2 * A_HEADS))).astype(BF16)
            b_if = jnp.pad(a_b_if[j], (0, LANES - 2 * A_HEADS)).reshape(1, LANES)
            q, k, v, o, gates = _a_in(x, ada_l, norm_g[i, 1], w_main, w_gates,
                                      a_conv_w[j], a_conv_b[j], b_if)
            y = _a_cell(q, k, v, o, gates, a_norm_g[j])
            w_out = a_w_out[j].astype(BF16)
        else:
            lam_init = 0.8 - 0.6 * math.exp(-0.3 * i)
            q, k, v = _b_in(x, ada_l, norm_g[i, 1], b_w_in[j].astype(BF16), cos_t, sin_t)
            y = _b_attn(q, k, v, b_lam[j], b_norm_g[j], lam_init)
            w_out = b_w_out[j].astype(BF16)
        x = _ffn(x, ada_l, norm_g[i, 2], w13_bf, w2_bf, (i, 1), final_g, sub=2,
                 final_norm=(i == depth - 1), mix=(y, w_out))
    return x
```

```python
import functools
import math

import jax
import jax.numpy as jnp
from jax import lax
from jax.experimental import pallas as pl
from jax.experimental.pallas import tpu as pltpu

F32 = jnp.float32
BF16 = jnp.bfloat16

EPS = 1e-6
N_MIXERS = 2
A_HEADS = 4
A_DK = 128
A_DV = 256
A_QK = A_HEADS * A_DK
A_V = A_HEADS * A_DV
A_CONV = 4
B_HEADS = 4
B_DH = 128
B_DV = 2 * B_DH
B_QK = B_HEADS * 2 * B_DH
B_V = B_HEADS * B_DV
ROPE_THETA = 500000.0
ROPE_DIM = B_DH // 4

LANES = 128
SUBLANES = 8
VMEM_LIMIT = 56 * 1024 * 1024
NEG_BIG = -1e30
LOG2E = math.log2(math.e)

FFN_TM = 1024
FFN_NSUB = 4
FFN_CW = 256
PROJ_TM = 512
PROJ_NSUB = 2
CELL_L = 256
CELL_NCH = 2
ATT_TQ = 1024


def _cparams(sem):
    return pltpu.CompilerParams(dimension_semantics=sem, vmem_limit_bytes=VMEM_LIMIT)


def _rms(x, g):
    return (x * lax.rsqrt(jnp.mean(x * x, axis=-1, keepdims=True) + EPS)) * g


def _modulate(x, g, shift, scale):
    return _rms(x, g) * (1.0 + scale) + shift


def _ada_rows(ada_ref, sub):
    return (ada_ref[0, 3 * sub:3 * sub + 1, :],
            ada_ref[0, 3 * sub + 1:3 * sub + 2, :],
            ada_ref[0, 3 * sub + 2:3 * sub + 3, :])


def _ada_kernel(c_ref, w_ref, b_ref, o_ref):
    c = c_ref[...]
    cond = (c * jax.nn.sigmoid(c)).astype(BF16)
    o_ref[0] = jnp.dot(cond, w_ref[0].astype(BF16), preferred_element_type=F32) + b_ref[0]


def _ada_all(c, ada_w, ada_b):
    depth, d, n = ada_w.shape
    nb = c.shape[0]
    rows = SUBLANES * pl.cdiv(nb, SUBLANES)
    cp = jnp.pad(c, ((0, rows - nb), (0, 0)))
    tn = 1024
    out = pl.pallas_call(
        _ada_kernel,
        grid=(depth, n // tn),
        in_specs=[pl.BlockSpec((rows, d), lambda l, j: (0, 0)),
                  pl.BlockSpec((1, d, tn), lambda l, j: (l, 0, j)),
                  pl.BlockSpec((1, 1, tn), lambda l, j: (l, 0, j))],
        out_specs=pl.BlockSpec((1, rows, tn), lambda l, j: (l, 0, j)),
        out_shape=jax.ShapeDtypeStruct((depth, rows, n), F32),
        compiler_params=_cparams(("parallel", "parallel")),
        name="ada",
    )(cp, ada_w, ada_b.reshape(depth, 1, n))
    return out[:, :nb].reshape(depth, nb, 9, d)


def _ffn_kernel(*refs, sub, final_norm, fuse_mix, nsub):
    if fuse_mix:
        x_ref, y_ref, wo_ref, ada_ref, g_ref, w1_ref, w3_ref, w2_ref, fg_ref, o_ref, act_scr = refs
    else:
        x_ref, ada_ref, g_ref, w1_ref, w3_ref, w2_ref, fg_ref, o_ref, act_scr = refs
    tm = x_ref.shape[1]
    dff = w2_ref.shape[0]
    tr = tm // nsub
    shift, scale, gt = _ada_rows(ada_ref, sub)
    for r in range(nsub):
        rows = slice(r * tr, (r + 1) * tr)
        x = x_ref[0, rows, :]
        if fuse_mix:
            _, _, gt_mix = _ada_rows(ada_ref, 1)
            x = x + gt_mix * jnp.dot(y_ref[0, rows, :], wo_ref[...], preferred_element_type=F32)
        o_ref[0, rows, :] = x
        h = _modulate(x, g_ref[...], shift, scale).astype(BF16)
        for c in range(dff // FFN_CW):
            cols = slice(c * FFN_CW, (c + 1) * FFN_CW)
            gate = jnp.dot(h, w1_ref[:, cols], preferred_element_type=F32)
            up = jnp.dot(h, w3_ref[:, cols], preferred_element_type=F32)
            act_scr[rows, cols] = ((gate * jax.nn.sigmoid(gate)) * up).astype(BF16)
        acc = jnp.dot(act_scr[rows, :], w2_ref[...], preferred_element_type=F32)
        y = o_ref[0, rows, :] + (0.5 * gt) * acc
        if final_norm:
            y = _rms(y, fg_ref[...])
        o_ref[0, rows, :] = y


def _ffn(x, ada_l, g, w13, w2, widx, final_g, *, sub, final_norm, mix=None):
    nb, s, d = x.shape
    dff = w2.shape[2]
    li, lj = widx
    tm = min(FFN_TM, s)
    assert dff % FFN_CW == 0 and s % tm == 0
    tok = lambda b, i: (b, i, 0)
    const = lambda b, i: (0, 0)
    once = dict(pipeline_mode=pl.Buffered(1))
    in_specs = [pl.BlockSpec((1, tm, d), tok)]
    args = [x]
    if mix is not None:
        y, w_out = mix
        dv = y.shape[-1]
        in_specs += [pl.BlockSpec((1, tm, dv), tok), pl.BlockSpec((dv, d), const, **once)]
        args += [y, w_out]
    in_specs += [pl.BlockSpec((1, 9, d), lambda b, i: (b, 0, 0)),
                 pl.BlockSpec((1, d), const),
                 pl.BlockSpec((None, None, d, dff), lambda b, i: (li, lj, 0, 0), **once),
                 pl.BlockSpec((None, None, d, dff), lambda b, i: (li, lj, 0, 1), **once),
                 pl.BlockSpec((None, None, dff, d), lambda b, i: (li, lj, 0, 0), **once),
                 pl.BlockSpec((1, d), const)]
    args += [ada_l, g.reshape(1, d), w13, w13, w2, final_g.reshape(1, d)]
    return pl.pallas_call(
        functools.partial(_ffn_kernel, sub=sub, final_norm=final_norm,
                          fuse_mix=mix is not None, nsub=FFN_NSUB),
        grid=(nb, s // tm),
        in_specs=in_specs,
        out_specs=pl.BlockSpec((1, tm, d), tok),
        out_shape=jax.ShapeDtypeStruct(x.shape, F32),
        scratch_shapes=[pltpu.VMEM((tm, dff), BF16)],
        compiler_params=_cparams(("parallel", "arbitrary")),
        name="ffn",
    )(*args)


def _a_in_kernel(x_ref, xh_ref, ada_ref, g_ref, w_ref, wg_ref, cw_ref, cb_ref, bif_ref,
                 q_ref, k_ref, v_ref, o_ref, gates_ref, ext_scr):
    i = pl.program_id(1)
    tm = x_ref.shape[1]
    halo = xh_ref.shape[1]
    shift, scale, _ = _ada_rows(ada_ref, 1)
    g = g_ref[...]
    nqk = 2 * A_QK
    wqk = w_ref[:, 0:nqk]
    hh = _modulate(xh_ref[0], g, shift, scale).astype(BF16)
    pre_h = jnp.dot(hh, wqk, preferred_element_type=F32)
    ext_scr[0:halo, :] = jnp.where(i == 0, 0.0, pre_h)

    tr = tm // PROJ_NSUB
    for r in range(PROJ_NSUB):
        rows = slice(r * tr, (r + 1) * tr)
        h = _modulate(x_ref[0, rows, :], g, shift, scale).astype(BF16)
        pre = jnp.dot(h, wqk, preferred_element_type=F32)
        ext_scr[halo + r * tr:halo + (r + 1) * tr, :] = pre
        conv = cb_ref[...] + cw_ref[A_CONV - 1:A_CONV, :] * pre
        for j in range(A_CONV - 1):
            off = halo - (A_CONV - 1) + j + r * tr
            conv = conv + cw_ref[j:j + 1, :] * ext_scr[off:off + tr, :]
        qk = conv * jax.nn.sigmoid(conv)
        q_ref[0, rows, :] = qk[:, 0:A_QK].astype(BF16)
        k_ref[0, rows, :] = (qk[:, A_QK:nqk] * (A_DK ** -0.5)).astype(BF16)

        v_ref[0, rows, :] = jnp.dot(h, w_ref[:, nqk:nqk + A_V],
                                    preferred_element_type=F32).astype(BF16)
        o_ref[0, rows, :] = jnp.dot(h, w_ref[:, nqk + A_V:nqk + 2 * A_V],
                                    preferred_element_type=F32)

        gp = jnp.dot(h, wg_ref[...], preferred_element_type=F32) + bif_ref[...]
        lane = lax.broadcasted_iota(jnp.int32, gp.shape, 1)
        gates_ref[0, rows, :] = jnp.where(lane < A_HEADS, gp, jax.nn.log_sigmoid(gp))


def _a_in(x, ada_l, g, w_main, w_gates, conv_w, conv_b, b_if):
    nb, s, d = x.shape
    tm = min(PROJ_TM, s)
    halo = SUBLANES
    nqk = 2 * A_QK
    hb = tm // halo
    tok = lambda b, i: (b, i, 0)
    return pl.pallas_call(
        _a_in_kernel,
        grid=(nb, s // tm),
        in_specs=[pl.BlockSpec((1, tm, d), tok),
                  pl.BlockSpec((1, halo, d), lambda b, i: (b, jnp.maximum(i * hb - 1, 0), 0)),
                  pl.BlockSpec((1, 9, d), lambda b, i: (b, 0, 0)),
                  pl.BlockSpec((1, d), lambda b, i: (0, 0)),
                  pl.BlockSpec(w_main.shape, lambda b, i: (0, 0)),
                  pl.BlockSpec(w_gates.shape, lambda b, i: (0, 0)),
                  pl.BlockSpec((A_CONV, nqk), lambda b, i: (0, 0)),
                  pl.BlockSpec((1, nqk), lambda b, i: (0, 0)),
                  pl.BlockSpec((1, LANES), lambda b, i: (0, 0))],
        out_specs=[pl.BlockSpec((1, tm, A_QK), tok),
                   pl.BlockSpec((1, tm, A_QK), tok),
                   pl.BlockSpec((1, tm, A_V), tok),
                   pl.BlockSpec((1, tm, A_V), tok),
                   pl.BlockSpec((1, tm, LANES), tok)],
        out_shape=[jax.ShapeDtypeStruct((nb, s, A_QK), BF16),
                   jax.ShapeDtypeStruct((nb, s, A_QK), BF16),
                   jax.ShapeDtypeStruct((nb, s, A_V), BF16),
                   jax.ShapeDtypeStruct((nb, s, A_V), F32),
                   jax.ShapeDtypeStruct((nb, s, LANES), F32)],
        scratch_shapes=[pltpu.VMEM((halo + tm, nqk), F32)],
        compiler_params=_cparams(("parallel", "arbitrary")),
        name="mlstm_in",
    )(x, x, ada_l, g.reshape(1, d), w_main, w_gates, conv_w, conv_b.reshape(1, nqk), b_if)


def _scan_rows(x, op, fill):
    n = x.shape[0]
    row = lax.broadcasted_iota(jnp.int32, x.shape, 0)
    d = 1
    while d < n:
        x = op(x, jnp.where(row >= d, pltpu.roll(x, d, axis=0), fill))
        d *= 2
    return x


def _a_cell_kernel(q_ref, k_ref, v_ref, o_ref, gates_ref, ng_ref, y_ref,
                   c_scr, n_scr, m_scr, *, L):

    nbk = q_ref.shape[0]

    @pl.when(pl.program_id(0) == 0)
    def _():
        c_scr[...] = jnp.zeros_like(c_scr)
        n_scr[...] = jnp.zeros_like(n_scr)
        m_scr[...] = jnp.zeros_like(m_scr)

    row = lax.broadcasted_iota(jnp.int32, (L, L), 0)
    col = lax.broadcasted_iota(jnp.int32, (L, L), 1)
    causal = col <= row

    def gate_terms(bi, rows):
        gates = gates_ref[bi, rows, :]
        log_f = pltpu.roll(gates, LANES - A_HEADS, axis=1)
        b = _scan_rows(log_f, jnp.add, 0.0)
        g = gates - b
        m_prev = m_scr[bi]
        big_m = jnp.maximum(m_prev, _scan_rows(g, jnp.maximum, NEG_BIG))
        inter = jnp.exp(m_prev - big_m)
        m_t = b + big_m
        m_scr[bi] = m_t[L - 1:L, :]
        return dict(big_m=big_m, inter=inter, floor=jnp.exp(-m_t),
                    w_col=jnp.exp(g - big_m[L - 1:L, :]), decay=inter[L - 1:L, :],
                    g_t=jnp.transpose(g))

    def head(bi, h, rows, t):
        qh = q_ref[bi, rows, h * A_DK:(h + 1) * A_DK]
        kh = k_ref[bi, rows, h * A_DK:(h + 1) * A_DK]
        vh = v_ref[bi, rows, h * A_DV:(h + 1) * A_DV]
        st = bi * A_HEADS + h
        dw = jnp.exp(jnp.where(causal, t["g_t"][h:h + 1, :] - t["big_m"][:, h:h + 1], NEG_BIG))
        s = lax.dot_general(qh, kh, (((1,), (1,)), ((), ())), preferred_element_type=F32) * dw
        inter_h = t["inter"][:, h:h + 1]
        c_h = c_scr[st]
        num = inter_h * jnp.dot(qh, c_h.astype(BF16), preferred_element_type=F32) \
            + jnp.dot(s.astype(BF16), vh, preferred_element_type=F32)
        qn = jnp.sum(qh.astype(F32) * n_scr[st], axis=-1, keepdims=True)
        den = inter_h * qn + jnp.sum(s, axis=-1, keepdims=True)
        hv = num / jnp.maximum(jnp.abs(den), t["floor"][:, h:h + 1])
        hv = hv * lax.rsqrt(jnp.mean(hv * hv, axis=-1, keepdims=True) + EPS)
        og = o_ref[bi, rows, h * A_DV:(h + 1) * A_DV]
        y = jax.nn.sigmoid(og) * (hv * ng_ref[:, h * A_DV:(h + 1) * A_DV])
        y_ref[bi, rows, h * A_DV:(h + 1) * A_DV] = y.astype(BF16)

        kw = kh.astype(F32) * t["w_col"][:, h:h + 1]
        dec = t["decay"][:, h:h + 1]
        c_scr[st] = dec * c_h + jnp.dot(jnp.transpose(kw).astype(BF16), vh,
                                        preferred_element_type=F32)
        n_scr[st] = dec * n_scr[st] + jnp.sum(kw, axis=0, keepdims=True)

    for r in range(q_ref.shape[1] // L):
        rows = slice(r * L, (r + 1) * L)
        terms = [gate_terms(bi, rows) for bi in range(nbk)]
        for h in range(A_HEADS):
            for bi in range(nbk):
                head(bi, h, rows, terms[bi])


def _a_cell(q, k, v, o, gates, norm_g):
    nb, s, _ = q.shape
    chunk = min(CELL_L, s)
    L = min(CELL_L * CELL_NCH, s)
    tok = lambda c: (0, c, 0)
    return pl.pallas_call(
        functools.partial(_a_cell_kernel, L=chunk),
        grid=(s // L,),
        in_specs=[pl.BlockSpec((nb, L, A_QK), tok),
                  pl.BlockSpec((nb, L, A_QK), tok),
                  pl.BlockSpec((nb, L, A_V), tok),
                  pl.BlockSpec((nb, L, A_V), tok),
                  pl.BlockSpec((nb, L, LANES), tok),
                  pl.BlockSpec((1, A_V), lambda c: (0, 0))],
        out_specs=pl.BlockSpec((nb, L, A_V), tok),
        out_shape=jax.ShapeDtypeStruct((nb, s, A_V), BF16),
        scratch_shapes=[pltpu.VMEM((nb * A_HEADS, A_DK, A_DV), F32),
                        pltpu.VMEM((nb * A_HEADS, 1, A_DK), F32),
                        pltpu.VMEM((nb, 1, LANES), F32)],
        compiler_params=_cparams(("arbitrary",)),
        name="mlstm_cell",
    )(q, k, v, o, gates, norm_g.reshape(1, A_V))


def _rope_tab_kernel(pos_ref, invf_ref, cos_ref, sin_ref):
    ang = pos_ref[0].astype(F32) * invf_ref[...]
    lane = lax.broadcasted_iota(jnp.int32, ang.shape, 1)
    half = ROPE_DIM // 2
    sn = jnp.sin(ang)
    cos_ref[0] = jnp.where(lane < ROPE_DIM, jnp.cos(ang), 1.0)
    sin_ref[0] = jnp.where(lane < half, -sn, jnp.where(lane < ROPE_DIM, sn, 0.0))


def _rope_tables(positions):
    nb, s = positions.shape
    half = ROPE_DIM // 2
    inv_freq = ROPE_THETA ** (-jnp.arange(0, ROPE_DIM, 2, dtype=F32) / ROPE_DIM)
    invf = jnp.zeros((1, LANES), F32).at[0, :ROPE_DIM].set(jnp.tile(inv_freq, 2))
    tm = min(1024, s)
    tok = lambda b, i: (b, i, 0)
    return pl.pallas_call(
        _rope_tab_kernel,
        grid=(nb, s // tm),
        in_specs=[pl.BlockSpec((1, tm, 1), tok),
                  pl.BlockSpec((1, LANES), lambda b, i: (0, 0))],
        out_specs=[pl.BlockSpec((1, tm, LANES), tok), pl.BlockSpec((1, tm, LANES), tok)],
        out_shape=[jax.ShapeDtypeStruct((nb, s, LANES), F32)] * 2,
        compiler_params=_cparams(("parallel", "parallel")),
        name="rope_tables",
    )(positions.reshape(nb, s, 1), invf)


def _rope(x, cos_t, sin_t):
    half = ROPE_DIM // 2
    lane = lax.broadcasted_iota(jnp.int32, cos_t.shape, 1)
    outs = []
    for c in range(x.shape[1] // B_DH):
        xg = x[:, c * B_DH:(c + 1) * B_DH]
        swapped = jnp.where(lane < half, pltpu.roll(xg, B_DH - half, axis=1),
                            pltpu.roll(xg, half, axis=1))
        outs.append(xg * cos_t + swapped * sin_t)
    return jnp.concatenate(outs, axis=1)


def _b_in_kernel(x_ref, ada_ref, g_ref, w_ref, cos_ref, sin_ref, q_ref, k_ref, v_ref):
    shift, scale, _ = _ada_rows(ada_ref, 1)
    h = _modulate(x_ref[0], g_ref[...], shift, scale).astype(BF16)
    cos_t = cos_ref[0]
    sin_t = sin_ref[0]
    q = jnp.dot(h, w_ref[:, 0:B_QK], preferred_element_type=F32)
    q_ref[0] = (_rope(q, cos_t, sin_t) * (B_DH ** -0.5 * LOG2E)).astype(BF16)
    k = jnp.dot(h, w_ref[:, B_QK:2 * B_QK], preferred_element_type=F32)
    k_ref[0] = _rope(k, cos_t, sin_t).astype(BF16)
    v_ref[0] = jnp.dot(h, w_ref[:, 2 * B_QK:2 * B_QK + B_V],
                       preferred_element_type=F32).astype(BF16)


def _b_in(x, ada_l, g, w_in, cos_t, sin_t):
    nb, s, d = x.shape
    tm = min(PROJ_TM, s)
    tok = lambda b, i: (b, i, 0)
    return pl.pallas_call(
        _b_in_kernel,
        grid=(nb, s // tm),
        in_specs=[pl.BlockSpec((1, tm, d), tok),
                  pl.BlockSpec((1, 9, d), lambda b, i: (b, 0, 0)),
                  pl.BlockSpec((1, d), lambda b, i: (0, 0)),
                  pl.BlockSpec(w_in.shape, lambda b, i: (0, 0)),
                  pl.BlockSpec((1, tm, LANES), tok),
                  pl.BlockSpec((1, tm, LANES), tok)],
        out_specs=[pl.BlockSpec((1, tm, B_QK), tok),
                   pl.BlockSpec((1, tm, B_QK), tok),
                   pl.BlockSpec((1, tm, B_V), tok)],
        out_shape=[jax.ShapeDtypeStruct((nb, s, B_QK), BF16),
                   jax.ShapeDtypeStruct((nb, s, B_QK), BF16),
                   jax.ShapeDtypeStruct((nb, s, B_V), BF16)],
        compiler_params=_cparams(("parallel", "parallel")),
        name="attn_in",
    )(x, ada_l, g.reshape(1, d), w_in, cos_t, sin_t)


def _b_attn_kernel(q_ref, k_ref, v_ref, lam_ref, ng_ref, y_ref, m_scr, l_scr, acc_scr,
                   sa_scr, sb_scr, *, lam_init, tk):
    qi = pl.program_id(2)
    tq = q_ref.shape[1]
    nchunk = tk // LANES

    m_scr[...] = jnp.full_like(m_scr, NEG_BIG)
    l_scr[...] = jnp.zeros_like(l_scr)
    acc_scr[...] = jnp.zeros_like(acc_scr)

    def scores(j, dst):
        ks = pl.multiple_of(j * tk, tk)
        for c in range(2):
            qc = q_ref[0, :, c * B_DH:(c + 1) * B_DH]
            kc = k_ref[0, pl.ds(ks, tk), c * B_DH:(c + 1) * B_DH]
            dst[c] = lax.dot_general(qc, kc, (((1,), (1,)), ((), ())),
                                     preferred_element_type=F32)

    def update(j, src, masked):
        ks = pl.multiple_of(j * tk, tk)
        v = v_ref[0, pl.ds(ks, tk), :]
        if masked:
            row = lax.broadcasted_iota(jnp.int32, (tq, tk), 0)
            col = lax.broadcasted_iota(jnp.int32, (tq, tk), 1)
            keep = col <= row
        for c in range(2):
            s = src[c]
            if masked:
                s = jnp.where(keep, s, NEG_BIG)
            m_old = m_scr[c]
            m_new = jnp.maximum(m_old, jnp.max(s, axis=-1, keepdims=True))
            alpha = jnp.exp2(m_old - m_new)
            chunks = [jnp.exp2(s[:, n * LANES:(n + 1) * LANES] - m_new) for n in range(nchunk)]
            part = chunks[0]
            for n in range(1, nchunk):
                part = part + chunks[n]
            l_scr[c] = alpha * l_scr[c] + part
            pr = jnp.concatenate(chunks, axis=1).astype(BF16)
            alpha_v = jnp.concatenate([alpha] * (B_DV // LANES), axis=1)
            acc_scr[c] = alpha_v * acc_scr[c] + jnp.dot(pr, v, preferred_element_type=F32)
            m_scr[c] = m_new

    buf_a = sa_scr
    buf_b = sb_scr
    scores(0, buf_a)

    def body(i, carry):
        j = 2 * i
        scores(j + 1, buf_b)
        update(j, buf_a, False)
        scores(j + 2, buf_a)
        update(j + 1, buf_b, False)
        return carry

    lax.fori_loop(0, qi // 2, body, 0)

    @pl.when(qi % 2 == 1)
    def _():
        scores(qi, buf_b)
        update(qi - 1, buf_a, False)
        update(qi, buf_b, True)

    @pl.when(qi % 2 == 0)
    def _():
        update(qi, buf_a, True)

    lf = lam_ref[...]
    lam_full = (jnp.exp(jnp.sum(lf[0:1] * lf[1:2], axis=-1, keepdims=True))
                - jnp.exp(jnp.sum(lf[2:3] * lf[3:4], axis=-1, keepdims=True)) + lam_init)
    l0 = jnp.sum(l_scr[0], axis=-1, keepdims=True)
    l1 = jnp.sum(l_scr[1], axis=-1, keepdims=True)
    o = acc_scr[0] / l0 - lam_full * (acc_scr[1] / l1)
    o = o * lax.rsqrt(jnp.mean(o * o, axis=-1, keepdims=True) + EPS)
    y_ref[0] = ((o * ng_ref[...]) * (1.0 - lam_init)).astype(BF16)


def _b_attn(q, k, v, lam, norm_g, lam_init):
    nb, s, _ = q.shape
    tq = min(ATT_TQ, s)
    tk = tq
    return pl.pallas_call(
        functools.partial(_b_attn_kernel, lam_init=lam_init, tk=tk),
        grid=(nb, B_HEADS, s // tq),
        in_specs=[pl.BlockSpec((1, tq, 2 * B_DH), lambda b, h, i: (b, i, h)),
                  pl.BlockSpec((1, s, 2 * B_DH), lambda b, h, i: (b, 0, h)),
                  pl.BlockSpec((1, s, B_DV), lambda b, h, i: (b, 0, h)),
                  pl.BlockSpec((4, B_DH), lambda b, h, i: (0, 0)),
                  pl.BlockSpec((1, B_DV), lambda b, h, i: (0, 0))],
        out_specs=pl.BlockSpec((1, tq, B_DV), lambda b, h, i: (b, i, h)),
        out_shape=jax.ShapeDtypeStruct((nb, s, B_V), BF16),
        scratch_shapes=[pltpu.VMEM((2, tq, LANES), F32),
                        pltpu.VMEM((2, tq, LANES), F32),
                        pltpu.VMEM((2, tq, B_DV), F32),
                        pltpu.VMEM((2, tq, tk), F32),
                        pltpu.VMEM((2, tq, tk), F32)],
        compiler_params=_cparams(("parallel", "parallel", "arbitrary")),
        name="diff_attn",
    )(q, k, v, lam, norm_g.reshape(1, B_DV))


def kernel(x, c, positions, ada_w, ada_b, norm_g, ffn_w13, ffn_w2, a_w_in, a_conv_w, a_conv_b,
           a_b_if, a_norm_g, a_w_out, b_w_in, b_lam, b_norm_g, b_w_out, final_g):
    depth = ada_w.shape[0]
    d = x.shape[-1]
    ada = _ada_all(c, ada_w, ada_b)
    cos_t, sin_t = _rope_tables(positions)

    nqk = 2 * A_QK
    n_main = nqk + 2 * A_V
    w13_bf = ffn_w13.astype(BF16)
    w2_bf = ffn_w2.astype(BF16)
    for i in range(depth):
        ada_l = ada[i]
        x = _ffn(x, ada_l, norm_g[i, 0], w13_bf, w2_bf, (i, 0), final_g, sub=0, final_norm=False)
        j = i // N_MIXERS
        if i % N_MIXERS == 0:
            w_in = a_w_in[j]
            w_main = w_in[:, :n_main].astype(BF16)
            w_gates = jnp.pad(w_in[:, n_main:], ((0, 0), (0, LANES - 2 * A_HEADS))).astype(BF16)
            b_if = jnp.pad(a_b_if[j], (0, LANES - 2 * A_HEADS)).reshape(1, LANES)
            q, k, v, o, gates = _a_in(x, ada_l, norm_g[i, 1], w_main, w_gates,
                                      a_conv_w[j], a_conv_b[j], b_if)
            y = _a_cell(q, k, v, o, gates, a_norm_g[j])
            w_out = a_w_out[j].astype(BF16)
        else:
            lam_init = 0.8 - 0.6 * math.exp(-0.3 * i)
            q, k, v = _b_in(x, ada_l, norm_g[i, 1], b_w_in[j].astype(BF16), cos_t, sin_t)
            y = _b_attn(q, k, v, b_lam[j], b_norm_g[j], lam_init)
            w_out = b_w_out[j].astype(BF16)
        x = _ffn(x, ada_l, norm_g[i, 2], w13_bf, w2_bf, (i, 1), final_g, sub=2,
                 final_norm=(i == depth - 1), mix=(y, w_out))
    return x
```

```python
import functools
import math

import jax
import jax.numpy as jnp
from jax import lax
from jax.experimental import pallas as pl
from jax.experimental.pallas import tpu as pltpu

F32 = jnp.float32
BF16 = jnp.bfloat16

EPS = 1e-6
N_MIXERS = 2
A_HEADS = 4
A_DK = 128
A_DV = 256
A_QK = A_HEADS * A_DK
A_V = A_HEADS * A_DV
A_CONV = 4
B_HEADS = 4
B_DH = 128
B_DV = 2 * B_DH
B_QK = B_HEADS * 2 * B_DH
B_V = B_HEADS * B_DV
ROPE_THETA = 500000.0
ROPE_DIM = B_DH // 4

LANES = 128
SUBLANES = 8
VMEM_LIMIT = 56 * 1024 * 1024
NEG_BIG = -1e30
LOG2E = math.log2(math.e)

FFN_TM = 1024
FFN_NSUB = 4
FFN_CW = 256
PROJ_TM = 1024
PROJ_NSUB = 4
CELL_L = 256
CELL_NCH = 2
ATT_TQ = 1024


def _cparams(sem):
    return pltpu.CompilerParams(dimension_semantics=sem, vmem_limit_bytes=VMEM_LIMIT)


def _rms(x, g):
    return (x * lax.rsqrt(jnp.mean(x * x, axis=-1, keepdims=True) + EPS)) * g


def _modulate(x, g, shift, scale):
    return _rms(x, g) * (1.0 + scale) + shift


def _ada_rows(ada_ref, sub):
    return (ada_ref[0, 3 * sub:3 * sub + 1, :],
            ada_ref[0, 3 * sub + 1:3 * sub + 2, :],
            ada_ref[0, 3 * sub + 2:3 * sub + 3, :])


def _ada_kernel(c_ref, w_ref, b_ref, o_ref):
    c = c_ref[...]
    cond = (c * jax.nn.sigmoid(c)).astype(BF16)
    o_ref[0] = jnp.dot(cond, w_ref[0].astype(BF16), preferred_element_type=F32) + b_ref[0]


def _ada_all(c, ada_w, ada_b):
    depth, d, n = ada_w.shape
    nb = c.shape[0]
    rows = SUBLANES * pl.cdiv(nb, SUBLANES)
    cp = jnp.pad(c, ((0, rows - nb), (0, 0)))
    tn = 1024
    out = pl.pallas_call(
        _ada_kernel,
        grid=(depth, n // tn),
        in_specs=[pl.BlockSpec((rows, d), lambda l, j: (0, 0)),
                  pl.BlockSpec((1, d, tn), lambda l, j: (l, 0, j)),
                  pl.BlockSpec((1, 1, tn), lambda l, j: (l, 0, j))],
        out_specs=pl.BlockSpec((1, rows, tn), lambda l, j: (l, 0, j)),
        out_shape=jax.ShapeDtypeStruct((depth, rows, n), F32),
        compiler_params=_cparams(("parallel", "parallel")),
        name="ada",
    )(cp, ada_w, ada_b.reshape(depth, 1, n))
    return out[:, :nb].reshape(depth, nb, 9, d)


def _ffn_kernel(*refs, sub, final_norm, fuse_mix, nsub):
    if fuse_mix:
        x_ref, y_ref, wo_ref, ada_ref, g_ref, w1_ref, w3_ref, w2_ref, fg_ref, o_ref, act_scr = refs
    else:
        x_ref, ada_ref, g_ref, w1_ref, w3_ref, w2_ref, fg_ref, o_ref, act_scr = refs
    tm = x_ref.shape[1]
    dff = w2_ref.shape[0]
    tr = tm // nsub
    shift, scale, gt = _ada_rows(ada_ref, sub)
    for r in range(nsub):
        rows = slice(r * tr, (r + 1) * tr)
        x = x_ref[0, rows, :]
        if fuse_mix:
            _, _, gt_mix = _ada_rows(ada_ref, 1)
            x = x + gt_mix * jnp.dot(y_ref[0, rows, :], wo_ref[...], preferred_element_type=F32)
        o_ref[0, rows, :] = x
        h = _modulate(x, g_ref[...], shift, scale).astype(BF16)
        for c in range(dff // FFN_CW):
            cols = slice(c * FFN_CW, (c + 1) * FFN_CW)
            gate = jnp.dot(h, w1_ref[:, cols], preferred_element_type=F32)
            up = jnp.dot(h, w3_ref[:, cols], preferred_element_type=F32)
            act_scr[rows, cols] = ((gate * jax.nn.sigmoid(gate)) * up).astype(BF16)
        acc = jnp.dot(act_scr[rows, :], w2_ref[...], preferred_element_type=F32)
        y = o_ref[0, rows, :] + (0.5 * gt) * acc
        if final_norm:
            y = _rms(y, fg_ref[...])
        o_ref[0, rows, :] = y


def _ffn(x, ada_l, g, w13, w2, widx, final_g, *, sub, final_norm, mix=None):
    nb, s, d = x.shape
    dff = w2.shape[2]
    li, lj = widx
    tm = min(FFN_TM, s)
    assert dff % FFN_CW == 0 and s % tm == 0
    tok = lambda b, i: (b, i, 0)
    const = lambda b, i: (0, 0)
    once = dict(pipeline_mode=pl.Buffered(1))
    in_specs = [pl.BlockSpec((1, tm, d), tok)]
    args = [x]
    if mix is not None:
        y, w_out = mix
        dv = y.shape[-1]
        in_specs += [pl.BlockSpec((1, tm, dv), tok), pl.BlockSpec((dv, d), const, **once)]
        args += [y, w_out]
    in_specs += [pl.BlockSpec((1, 9, d), lambda b, i: (b, 0, 0)),
                 pl.BlockSpec((1, d), const),
                 pl.BlockSpec((None, None, d, dff), lambda b, i: (li, lj, 0, 0), **once),
                 pl.BlockSpec((None, None, d, dff), lambda b, i: (li, lj, 0, 1), **once),
                 pl.BlockSpec((None, None, dff, d), lambda b, i: (li, lj, 0, 0), **once),
                 pl.BlockSpec((1, d), const)]
    args += [ada_l, g.reshape(1, d), w13, w13, w2, final_g.reshape(1, d)]
    return pl.pallas_call(
        functools.partial(_ffn_kernel, sub=sub, final_norm=final_norm,
                          fuse_mix=mix is not None, nsub=FFN_NSUB),
        grid=(nb, s // tm),
        in_specs=in_specs,
        out_specs=pl.BlockSpec((1, tm, d), tok),
        out_shape=jax.ShapeDtypeStruct(x.shape, F32),
        scratch_shapes=[pltpu.VMEM((tm, dff), BF16)],
        compiler_params=_cparams(("parallel", "arbitrary")),
        name="ffn",
    )(*args)


def _a_in_kernel(x_ref, xh_ref, ada_ref, g_ref, w_ref, wg_ref, cw_ref, cb_ref, bif_ref,
                 q_ref, k_ref, v_ref, o_ref, gates_ref, ext_scr):
    i = pl.program_id(1)
    tm = x_ref.shape[1]
    halo = xh_ref.shape[1]
    shift, scale, _ = _ada_rows(ada_ref, 1)
    g = g_ref[...]
    nqk = 2 * A_QK
    wqk = w_ref[:, 0:nqk]
    hh = _modulate(xh_ref[0], g, shift, scale).astype(BF16)
    pre_h = jnp.dot(hh, wqk, preferred_element_type=F32)
    ext_scr[0:halo, :] = jnp.where(i == 0, 0.0, pre_h)

    tr = tm // PROJ_NSUB
    for r in range(PROJ_NSUB):
        rows = slice(r * tr, (r + 1) * tr)
        h = _modulate(x_ref[0, rows, :], g, shift, scale).astype(BF16)
        pre = jnp.dot(h, wqk, preferred_element_type=F32)
        ext_scr[halo + r * tr:halo + (r + 1) * tr, :] = pre
        conv = cb_ref[...] + cw_ref[A_CONV - 1:A_CONV, :] * pre
        for j in range(A_CONV - 1):
            off = halo - (A_CONV - 1) + j + r * tr
            conv = conv + cw_ref[j:j + 1, :] * ext_scr[off:off + tr, :]
        qk = conv * jax.nn.sigmoid(conv)
        q_ref[0, rows, :] = qk[:, 0:A_QK].astype(BF16)
        k_ref[0, rows, :] = (qk[:, A_QK:nqk] * (A_DK ** -0.5)).astype(BF16)

        v_ref[0, rows, :] = jnp.dot(h, w_ref[:, nqk:nqk + A_V],
                                    preferred_element_type=F32).astype(BF16)
        o_ref[0, rows, :] = jnp.dot(h, w_ref[:, nqk + A_V:nqk + 2 * A_V],
                                    preferred_element_type=F32)

        gp = jnp.dot(h, wg_ref[...], preferred_element_type=F32) + bif_ref[...]
        lane = lax.broadcasted_iota(jnp.int32, gp.shape, 1)
        gates_ref[0, rows, :] = jnp.where(lane < A_HEADS, gp, jax.nn.log_sigmoid(gp))


def _a_in(x, ada_l, g, w_main, w_gates, conv_w, conv_b, b_if):
    nb, s, d = x.shape
    tm = min(PROJ_TM, s)
    halo = SUBLANES
    nqk = 2 * A_QK
    hb = tm // halo
    tok = lambda b, i: (b, i, 0)
    return pl.pallas_call(
        _a_in_kernel,
        grid=(nb, s // tm),
        in_specs=[pl.BlockSpec((1, tm, d), tok),
                  pl.BlockSpec((1, halo, d), lambda b, i: (b, jnp.maximum(i * hb - 1, 0), 0)),
                  pl.BlockSpec((1, 9, d), lambda b, i: (b, 0, 0)),
                  pl.BlockSpec((1, d), lambda b, i: (0, 0)),
                  pl.BlockSpec(w_main.shape, lambda b, i: (0, 0)),
                  pl.BlockSpec(w_gates.shape, lambda b, i: (0, 0)),
                  pl.BlockSpec((A_CONV, nqk), lambda b, i: (0, 0)),
                  pl.BlockSpec((1, nqk), lambda b, i: (0, 0)),
                  pl.BlockSpec((1, LANES), lambda b, i: (0, 0))],
        out_specs=[pl.BlockSpec((1, tm, A_QK), tok),
                   pl.BlockSpec((1, tm, A_QK), tok),
                   pl.BlockSpec((1, tm, A_V), tok),
                   pl.BlockSpec((1, tm, A_V), tok),
                   pl.BlockSpec((1, tm, LANES), tok)],
        out_shape=[jax.ShapeDtypeStruct((nb, s, A_QK), BF16),
                   jax.ShapeDtypeStruct((nb, s, A_QK), BF16),
                   jax.ShapeDtypeStruct((nb, s, A_V), BF16),
                   jax.ShapeDtypeStruct((nb, s, A_V), F32),
                   jax.ShapeDtypeStruct((nb, s, LANES), F32)],
        scratch_shapes=[pltpu.VMEM((halo + tm, nqk), F32)],
        compiler_params=_cparams(("parallel", "arbitrary")),
        name="mlstm_in",
    )(x, x, ada_l, g.reshape(1, d), w_main, w_gates, conv_w, conv_b.reshape(1, nqk), b_if)


def _scan_rows(x, op, fill):
    n = x.shape[0]
    row = lax.broadcasted_iota(jnp.int32, x.shape, 0)
    d = 1
    while d < n:
        x = op(x, jnp.where(row >= d, pltpu.roll(x, d, axis=0), fill))
        d *= 2
    return x


def _a_cell_kernel(q_ref, k_ref, v_ref, o_ref, gates_ref, ng_ref, y_ref,
                   c_scr, n_scr, m_scr, *, L):

    nbk = q_ref.shape[0]

    @pl.when(pl.program_id(0) == 0)
    def _():
        c_scr[...] = jnp.zeros_like(c_scr)
        n_scr[...] = jnp.zeros_like(n_scr)
        m_scr[...] = jnp.zeros_like(m_scr)

    row = lax.broadcasted_iota(jnp.int32, (L, L), 0)
    col = lax.broadcasted_iota(jnp.int32, (L, L), 1)
    causal = col <= row

    def gate_terms(bi, rows):
        gates = gates_ref[bi, rows, :]
        log_f = pltpu.roll(gates, LANES - A_HEADS, axis=1)
        b = _scan_rows(log_f, jnp.add, 0.0)
        g = gates - b
        m_prev = m_scr[bi]
        big_m = jnp.maximum(m_prev, _scan_rows(g, jnp.maximum, NEG_BIG))
        inter = jnp.exp(m_prev - big_m)
        m_t = b + big_m
        m_scr[bi] = m_t[L - 1:L, :]
        return dict(big_m=big_m, inter=inter, floor=jnp.exp(-m_t),
                    w_col=jnp.exp(g - big_m[L - 1:L, :]), decay=inter[L - 1:L, :],
                    g_t=jnp.transpose(g))

    def head(bi, h, rows, t):
        qh = q_ref[bi, rows, h * A_DK:(h + 1) * A_DK]
        kh = k_ref[bi, rows, h * A_DK:(h + 1) * A_DK]
        vh = v_ref[bi, rows, h * A_DV:(h + 1) * A_DV]
        st = bi * A_HEADS + h
        dw = jnp.exp(jnp.where(causal, t["g_t"][h:h + 1, :] - t["big_m"][:, h:h + 1], NEG_BIG))
        s = lax.dot_general(qh, kh, (((1,), (1,)), ((), ())), preferred_element_type=F32) * dw
        inter_h = t["inter"][:, h:h + 1]
        c_h = c_scr[st]
        num = inter_h * jnp.dot(qh, c_h.astype(BF16), preferred_element_type=F32) \
            + jnp.dot(s.astype(BF16), vh, preferred_element_type=F32)
        qn = jnp.sum(qh.astype(F32) * n_scr[st], axis=-1, keepdims=True)
        den = inter_h * qn + jnp.sum(s, axis=-1, keepdims=True)
        hv = num / jnp.maximum(jnp.abs(den), t["floor"][:, h:h + 1])
        hv = hv * lax.rsqrt(jnp.mean(hv * hv, axis=-1, keepdims=True) + EPS)
        og = o_ref[bi, rows, h * A_DV:(h + 1) * A_DV]
        y = jax.nn.sigmoid(og) * (hv * ng_ref[:, h * A_DV:(h + 1) * A_DV])
        y_ref[bi, rows, h * A_DV:(h + 1) * A_DV] = y.astype(BF16)

        kw = kh.astype(F32) * t["w_col"][:, h:h + 1]
        dec = t["decay"][:, h:h + 1]
        c_scr[st] = dec * c_h + jnp.dot(jnp.transpose(kw).astype(BF16), vh,
                                        preferred_element_type=F32)
        n_scr[st] = dec * n_scr[st] + jnp.sum(kw, axis=0, keepdims=True)

    for r in range(q_ref.shape[1] // L):
        rows = slice(r * L, (r + 1) * L)
        terms = [gate_terms(bi, rows) for bi in range(nbk)]
        for h in range(A_HEADS):
            for bi in range(nbk):
                head(bi, h, rows, terms[bi])


def _a_cell(q, k, v, o, gates, norm_g):
    nb, s, _ = q.shape
    chunk = min(CELL_L, s)
    L = min(CELL_L * CELL_NCH, s)
    tok = lambda c: (0, c, 0)
    return pl.pallas_call(
        functools.partial(_a_cell_kernel, L=chunk),
        grid=(s // L,),
        in_specs=[pl.BlockSpec((nb, L, A_QK), tok),
                  pl.BlockSpec((nb, L, A_QK), tok),
                  pl.BlockSpec((nb, L, A_V), tok),
                  pl.BlockSpec((nb, L, A_V), tok),
                  pl.BlockSpec((nb, L, LANES), tok),
                  pl.BlockSpec((1, A_V), lambda c: (0, 0))],
        out_specs=pl.BlockSpec((nb, L, A_V), tok),
        out_shape=jax.ShapeDtypeStruct((nb, s, A_V), BF16),
        scratch_shapes=[pltpu.VMEM((nb * A_HEADS, A_DK, A_DV), F32),
                        pltpu.VMEM((nb * A_HEADS, 1, A_DK), F32),
                        pltpu.VMEM((nb, 1, LANES), F32)],
        compiler_params=_cparams(("arbitrary",)),
        name="mlstm_cell",
    )(q, k, v, o, gates, norm_g.reshape(1, A_V))


def _rope_tab_kernel(pos_ref, invf_ref, cos_ref, sin_ref):
    ang = pos_ref[0].astype(F32) * invf_ref[...]
    lane = lax.broadcasted_iota(jnp.int32, ang.shape, 1)
    half = ROPE_DIM // 2
    sn = jnp.sin(ang)
    cos_ref[0] = jnp.where(lane < ROPE_DIM, jnp.cos(ang), 1.0)
    sin_ref[0] = jnp.where(lane < half, -sn, jnp.where(lane < ROPE_DIM, sn, 0.0))


def _rope_tables(positions):
    nb, s = positions.shape
    half = ROPE_DIM // 2
    inv_freq = ROPE_THETA ** (-jnp.arange(0, ROPE_DIM, 2, dtype=F32) / ROPE_DIM)
    invf = jnp.zeros((1, LANES), F32).at[0, :ROPE_DIM].set(jnp.tile(inv_freq, 2))
    tm = min(1024, s)
    tok = lambda b, i: (b, i, 0)
    return pl.pallas_call(
        _rope_tab_kernel,
        grid=(nb, s // tm),
        in_specs=[pl.BlockSpec((1, tm, 1), tok),
                  pl.BlockSpec((1, LANES), lambda b, i: (0, 0))],
        out_specs=[pl.BlockSpec((1, tm, LANES), tok), pl.BlockSpec((1, tm, LANES), tok)],
        out_shape=[jax.ShapeDtypeStruct((nb, s, LANES), F32)] * 2,
        compiler_params=_cparams(("parallel", "parallel")),
        name="rope_tables",
    )(positions.reshape(nb, s, 1), invf)


def _rope(x, cos_t, sin_t):
    half = ROPE_DIM // 2
    lane = lax.broadcasted_iota(jnp.int32, cos_t.shape, 1)
    outs = []
    for c in range(x.shape[1] // B_DH):
        xg = x[:, c * B_DH:(c + 1) * B_DH]
        swapped = jnp.where(lane < half, pltpu.roll(xg, B_DH - half, axis=1),
                            pltpu.roll(xg, half, axis=1))
        outs.append(xg * cos_t + swapped * sin_t)
    return jnp.concatenate(outs, axis=1)


def _b_in_kernel(x_ref, ada_ref, g_ref, w_ref, cos_ref, sin_ref, q_ref, k_ref, v_ref):
    shift, scale, _ = _ada_rows(ada_ref, 1)
    h = _modulate(x_ref[0], g_ref[...], shift, scale).astype(BF16)
    cos_t = cos_ref[0]
    sin_t = sin_ref[0]
    q = jnp.dot(h, w_ref[:, 0:B_QK], preferred_element_type=F32)
    q_ref[0] = (_rope(q, cos_t, sin_t) * (B_DH ** -0.5 * LOG2E)).astype(BF16)
    k = jnp.dot(h, w_ref[:, B_QK:2 * B_QK], preferred_element_type=F32)
    k_ref[0] = _rope(k, cos_t, sin_t).astype(BF16)
    v_ref[0] = jnp.dot(h, w_ref[:, 2 * B_QK:2 * B_QK + B_V],
                       preferred_element_type=F32).astype(BF16)


def _b_in(x, ada_l, g, w_in, cos_t, sin_t):
    nb, s, d = x.shape
    tm = min(PROJ_TM, s)
    tok = lambda b, i: (b, i, 0)
    return pl.pallas_call(
        _b_in_kernel,
        grid=(nb, s // tm),
        in_specs=[pl.BlockSpec((1, tm, d), tok),
                  pl.BlockSpec((1, 9, d), lambda b, i: (b, 0, 0)),
                  pl.BlockSpec((1, d), lambda b, i: (0, 0)),
                  pl.BlockSpec(w_in.shape, lambda b, i: (0, 0)),
                  pl.BlockSpec((1, tm, LANES), tok),
                  pl.BlockSpec((1, tm, LANES), tok)],
        out_specs=[pl.BlockSpec((1, tm, B_QK), tok),
                   pl.BlockSpec((1, tm, B_QK), tok),
                   pl.BlockSpec((1, tm, B_V), tok)],
        out_shape=[jax.ShapeDtypeStruct((nb, s, B_QK), BF16),
                   jax.ShapeDtypeStruct((nb, s, B_QK), BF16),
                   jax.ShapeDtypeStruct((nb, s, B_V), BF16)],
        compiler_params=_cparams(("parallel", "parallel")),
        name="attn_in",
    )(x, ada_l, g.reshape(1, d), w_in, cos_t, sin_t)


def _b_attn_kernel(q_ref, k_ref, v_ref, lam_ref, ng_ref, y_ref, m_scr, l_scr, acc_scr,
                   sa_scr, sb_scr, *, lam_init, tk):
    qi = pl.program_id(2)
    tq = q_ref.shape[1]
    nchunk = tk // LANES

    m_scr[...] = jnp.full_like(m_scr, NEG_BIG)
    l_scr[...] = jnp.zeros_like(l_scr)
    acc_scr[...] = jnp.zeros_like(acc_scr)

    def scores(j, dst):
        ks = pl.multiple_of(j * tk, tk)
        for c in range(2):
            qc = q_ref[0, :, c * B_DH:(c + 1) * B_DH]
            kc = k_ref[0, pl.ds(ks, tk), c * B_DH:(c + 1) * B_DH]
            dst[c] = lax.dot_general(qc, kc, (((1,), (1,)), ((), ())),
                                     preferred_element_type=F32)

    def update(j, src, masked):
        ks = pl.multiple_of(j * tk, tk)
        v = v_ref[0, pl.ds(ks, tk), :]
        if masked:
            row = lax.broadcasted_iota(jnp.int32, (tq, tk), 0)
            col = lax.broadcasted_iota(jnp.int32, (tq, tk), 1)
            keep = col <= row
        for c in range(2):
            s = src[c]
            if masked:
                s = jnp.where(keep, s, NEG_BIG)
            m_old = m_scr[c]
            m_new = jnp.maximum(m_old, jnp.max(s, axis=-1, keepdims=True))
            alpha = jnp.exp2(m_old - m_new)
            chunks = [jnp.exp2(s[:, n * LANES:(n + 1) * LANES] - m_new) for n in range(nchunk)]
            part = chunks[0]
            for n in range(1, nchunk):
                part = part + chunks[n]
            l_scr[c] = alpha * l_scr[c] + part
            pr = jnp.concatenate(chunks, axis=1).astype(BF16)
            alpha_v = jnp.concatenate([alpha] * (B_DV // LANES), axis=1)
            acc_scr[c] = alpha_v * acc_scr[c] + jnp.dot(pr, v, preferred_element_type=F32)
            m_scr[c] = m_new

    buf_a = sa_scr
    buf_b = sb_scr
    scores(0, buf_a)

    def body(i, carry):
        j = 2 * i
        scores(j + 1, buf_b)
        update(j, buf_a, False)
        scores(j + 2, buf_a)
        update(j + 1, buf_b, False)
        return carry

    lax.fori_loop(0, qi // 2, body, 0)

    @pl.when(qi % 2 == 1)
    def _():
        scores(qi, buf_b)
        update(qi - 1, buf_a, False)
        update(qi, buf_b, True)

    @pl.when(qi % 2 == 0)
    def _():
        update(qi, buf_a, True)

    lf = lam_ref[...]
    lam_full = (jnp.exp(jnp.sum(lf[0:1] * lf[1:2], axis=-1, keepdims=True))
                - jnp.exp(jnp.sum(lf[2:3] * lf[3:4], axis=-1, keepdims=True)) + lam_init)
    l0 = jnp.sum(l_scr[0], axis=-1, keepdims=True)
    l1 = jnp.sum(l_scr[1], axis=-1, keepdims=True)
    o = acc_scr[0] / l0 - lam_full * (acc_scr[1] / l1)
    o = o * lax.rsqrt(jnp.mean(o * o, axis=-1, keepdims=True) + EPS)
    y_ref[0] = ((o * ng_ref[...]) * (1.0 - lam_init)).astype(BF16)


def _b_attn(q, k, v, lam, norm_g, lam_init):
    nb, s, _ = q.shape
    tq = min(ATT_TQ, s)
    tk = tq
    return pl.pallas_call(
        functools.partial(_b_attn_kernel, lam_init=lam_init, tk=tk),
        grid=(nb, B_HEADS, s // tq),
        in_specs=[pl.BlockSpec((1, tq, 2 * B_DH), lambda b, h, i: (b, i, h)),
                  pl.BlockSpec((1, s, 2 * B_DH), lambda b, h, i: (b, 0, h)),
                  pl.BlockSpec((1, s, B_DV), lambda b, h, i: (b, 0, h)),
                  pl.BlockSpec((4, B_DH), lambda b, h, i: (0, 0)),
                  pl.BlockSpec((1, B_DV), lambda b, h, i: (0, 0))],
        out_specs=pl.BlockSpec((1, tq, B_DV), lambda b, h, i: (b, i, h)),
        out_shape=jax.ShapeDtypeStruct((nb, s, B_V), BF16),
        scratch_shapes=[pltpu.VMEM((2, tq, LANES), F32),
                        pltpu.VMEM((2, tq, LANES), F32),
                        pltpu.VMEM((2, tq, B_DV), F32),
                        pltpu.VMEM((2, tq, tk), F32),
                        pltpu.VMEM((2, tq, tk), F32)],
        compiler_params=_cparams(("parallel", "parallel", "arbitrary")),
        name="diff_attn",
    )(q, k, v, lam, norm_g.reshape(1, B_DV))


def kernel(x, c, positions, ada_w, ada_b, norm_g, ffn_w13, ffn_w2, a_w_in, a_conv_w, a_conv_b,
           a_b_if, a_norm_g, a_w_out, b_w_in, b_lam, b_norm_g, b_w_out, final_g):
    depth = ada_w.shape[0]
    d = x.shape[-1]
    ada = _ada_all(c, ada_w, ada_b)
    cos_t, sin_t = _rope_tables(positions)

    nqk = 2 * A_QK
    n_main = nqk + 2 * A_V
    w13_bf = ffn_w13.astype(BF16)
    w2_bf = ffn_w2.astype(BF16)
    for i in range(depth):
        ada_l = ada[i]
        x = _ffn(x, ada_l, norm_g[i, 0], w13_bf, w2_bf, (i, 0), final_g, sub=0, final_norm=False)
        j = i // N_MIXERS
        if i % N_MIXERS == 0:
            w_in = a_w_in[j]
            w_main = w_in[:, :n_main].astype(BF16)
            w_gates = jnp.pad(w_in[:, n_main:], ((0, 0), (0, LANES - 2 * A_HEADS))).astype(BF16)
            b_if = jnp.pad(a_b_if[j], (0, LANES - 2 * A_HEADS)).reshape(1, LANES)
            q, k, v, o, gates = _a_in(x, ada_l, norm_g[i, 1], w_main, w_gates,
                                      a_conv_w[j], a_conv_b[j], b_if)
            y = _a_cell(q, k, v, o, gates, a_norm_g[j])
            w_out = a_w_out[j].astype(BF16)
        else:
            lam_init = 0.8 - 0.6 * math.exp(-0.3 * i)
            q, k, v = _b_in(x, ada_l, norm_g[i, 1], b_w_in[j].astype(BF16), cos_t, sin_t)
            y = _b_attn(q, k, v, b_lam[j], b_norm_g[j], lam_init)
            w_out = b_w_out[j].astype(BF16)
        x = _ffn(x, ada_l, norm_g[i, 2], w13_bf, w2_bf, (i, 1), final_g, sub=2,
                 final_norm=(i == depth - 1), mix=(y, w_out))
    return x
```

```python
import functools
import math

import jax
import jax.numpy as jnp
from jax import lax
from jax.experimental import pallas as pl
from jax.experimental.pallas import tpu as pltpu

F32 = jnp.float32
BF16 = jnp.bfloat16

EPS = 1e-6
N_MIXERS = 2
A_HEADS = 4
A_DK = 128
A_DV = 256
A_QK = A_HEADS * A_DK
A_V = A_HEADS * A_DV
A_CONV = 4
B_HEADS = 4
B_DH = 128
B_DV = 2 * B_DH
B_QK = B_HEADS * 2 * B_DH
B_V = B_HEADS * B_DV
ROPE_THETA = 500000.0
ROPE_DIM = B_DH // 4

LANES = 128
SUBLANES = 8
VMEM_LIMIT = 56 * 1024 * 1024
NEG_BIG = -1e30
LOG2E = math.log2(math.e)

FFN_TM = 1024
FFN_NSUB = 4
FFN_CW = 256
PROJ_TM = 1024
PROJ_NSUB = 4
CELL_L = 512
CELL_NCH = 1
ATT_TQ = 1024


def _cparams(sem):
    return pltpu.CompilerParams(dimension_semantics=sem, vmem_limit_bytes=VMEM_LIMIT)


def _rms(x, g):
    return (x * lax.rsqrt(jnp.mean(x * x, axis=-1, keepdims=True) + EPS)) * g


def _modulate(x, g, shift, scale):
    return _rms(x, g) * (1.0 + scale) + shift


def _ada_rows(ada_ref, sub):
    return (ada_ref[0, 3 * sub:3 * sub + 1, :],
            ada_ref[0, 3 * sub + 1:3 * sub + 2, :],
            ada_ref[0, 3 * sub + 2:3 * sub + 3, :])


def _ada_kernel(c_ref, w_ref, b_ref, o_ref):
    c = c_ref[...]
    cond = (c * jax.nn.sigmoid(c)).astype(BF16)
    o_ref[0] = jnp.dot(cond, w_ref[0].astype(BF16), preferred_element_type=F32) + b_ref[0]


def _ada_all(c, ada_w, ada_b):
    depth, d, n = ada_w.shape
    nb = c.shape[0]
    rows = SUBLANES * pl.cdiv(nb, SUBLANES)
    cp = jnp.pad(c, ((0, rows - nb), (0, 0)))
    tn = 1024
    out = pl.pallas_call(
        _ada_kernel,
        grid=(depth, n // tn),
        in_specs=[pl.BlockSpec((rows, d), lambda l, j: (0, 0)),
                  pl.BlockSpec((1, d, tn), lambda l, j: (l, 0, j)),
                  pl.BlockSpec((1, 1, tn), lambda l, j: (l, 0, j))],
        out_specs=pl.BlockSpec((1, rows, tn), lambda l, j: (l, 0, j)),
        out_shape=jax.ShapeDtypeStruct((depth, rows, n), F32),
        compiler_params=_cparams(("parallel", "parallel")),
        name="ada",
    )(cp, ada_w, ada_b.reshape(depth, 1, n))
    return out[:, :nb].reshape(depth, nb, 9, d)


def _ffn_kernel(*refs, sub, final_norm, fuse_mix, nsub):
    if fuse_mix:
        x_ref, y_ref, wo_ref, ada_ref, g_ref, w1_ref, w3_ref, w2_ref, fg_ref, o_ref, act_scr = refs
    else:
        x_ref, ada_ref, g_ref, w1_ref, w3_ref, w2_ref, fg_ref, o_ref, act_scr = refs
    tm = x_ref.shape[1]
    dff = w2_ref.shape[0]
    tr = tm // nsub
    shift, scale, gt = _ada_rows(ada_ref, sub)
    for r in range(nsub):
        rows = slice(r * tr, (r + 1) * tr)
        x = x_ref[0, rows, :]
        if fuse_mix:
            _, _, gt_mix = _ada_rows(ada_ref, 1)
            x = x + gt_mix * jnp.dot(y_ref[0, rows, :], wo_ref[...], preferred_element_type=F32)
        o_ref[0, rows, :] = x
        h = _modulate(x, g_ref[...], shift, scale).astype(BF16)
        for c in range(dff // FFN_CW):
            cols = slice(c * FFN_CW, (c + 1) * FFN_CW)
            gate = jnp.dot(h, w1_ref[:, cols], preferred_element_type=F32)
            up = jnp.dot(h, w3_ref[:, cols], preferred_element_type=F32)
            act_scr[rows, cols] = ((gate * jax.nn.sigmoid(gate)) * up).astype(BF16)
        acc = jnp.dot(act_scr[rows, :], w2_ref[...], preferred_element_type=F32)
        y = o_ref[0, rows, :] + (0.5 * gt) * acc
        if final_norm:
            y = _rms(y, fg_ref[...])
        o_ref[0, rows, :] = y


def _ffn(x, ada_l, g, w13, w2, widx, final_g, *, sub, final_norm, mix=None):
    nb, s, d = x.shape
    dff = w2.shape[2]
    li, lj = widx
    tm = min(FFN_TM, s)
    assert dff % FFN_CW == 0 and s % tm == 0
    tok = lambda b, i: (b, i, 0)
    const = lambda b, i: (0, 0)
    once = dict(pipeline_mode=pl.Buffered(1))
    in_specs = [pl.BlockSpec((1, tm, d), tok)]
    args = [x]
    if mix is not None:
        y, w_out = mix
        dv = y.shape[-1]
        in_specs += [pl.BlockSpec((1, tm, dv), tok), pl.BlockSpec((dv, d), const, **once)]
        args += [y, w_out]
    in_specs += [pl.BlockSpec((1, 9, d), lambda b, i: (b, 0, 0)),
                 pl.BlockSpec((1, d), const),
                 pl.BlockSpec((None, None, d, dff), lambda b, i: (li, lj, 0, 0), **once),
                 pl.BlockSpec((None, None, d, dff), lambda b, i: (li, lj, 0, 1), **once),
                 pl.BlockSpec((None, None, dff, d), lambda b, i: (li, lj, 0, 0), **once),
                 pl.BlockSpec((1, d), const)]
    args += [ada_l, g.reshape(1, d), w13, w13, w2, final_g.reshape(1, d)]
    return pl.pallas_call(
        functools.partial(_ffn_kernel, sub=sub, final_norm=final_norm,
                          fuse_mix=mix is not None, nsub=FFN_NSUB),
        grid=(nb, s // tm),
        in_specs=in_specs,
        out_specs=pl.BlockSpec((1, tm, d), tok),
        out_shape=jax.ShapeDtypeStruct(x.shape, F32),
        scratch_shapes=[pltpu.VMEM((tm, dff), BF16)],
        compiler_params=_cparams(("parallel", "arbitrary")),
        name="ffn",
    )(*args)


def _a_in_kernel(x_ref, xh_ref, ada_ref, g_ref, w_ref, wg_ref, cw_ref, cb_ref, bif_ref,
                 q_ref, k_ref, v_ref, o_ref, gates_ref, ext_scr):
    i = pl.program_id(1)
    tm = x_ref.shape[1]
    halo = xh_ref.shape[1]
    shift, scale, _ = _ada_rows(ada_ref, 1)
    g = g_ref[...]
    nqk = 2 * A_QK
    wqk = w_ref[:, 0:nqk]
    hh = _modulate(xh_ref[0], g, shift, scale).astype(BF16)
    pre_h = jnp.dot(hh, wqk, preferred_element_type=F32)
    ext_scr[0:halo, :] = jnp.where(i == 0, 0.0, pre_h)

    tr = tm // PROJ_NSUB
    for r in range(PROJ_NSUB):
        rows = slice(r * tr, (r + 1) * tr)
        h = _modulate(x_ref[0, rows, :], g, shift, scale).astype(BF16)
        pre = jnp.dot(h, wqk, preferred_element_type=F32)
        ext_scr[halo + r * tr:halo + (r + 1) * tr, :] = pre
        conv = cb_ref[...] + cw_ref[A_CONV - 1:A_CONV, :] * pre
        for j in range(A_CONV - 1):
            off = halo - (A_CONV - 1) + j + r * tr
            conv = conv + cw_ref[j:j + 1, :] * ext_scr[off:off + tr, :]
        qk = conv * jax.nn.sigmoid(conv)
        q_ref[0, rows, :] = qk[:, 0:A_QK].astype(BF16)
        k_ref[0, rows, :] = (qk[:, A_QK:nqk] * (A_DK ** -0.5)).astype(BF16)

        v_ref[0, rows, :] = jnp.dot(h, w_ref[:, nqk:nqk + A_V],
                                    preferred_element_type=F32).astype(BF16)
        o_ref[0, rows, :] = jnp.dot(h, w_ref[:, nqk + A_V:nqk + 2 * A_V],
                                    preferred_element_type=F32)

        gp = jnp.dot(h, wg_ref[...], preferred_element_type=F32) + bif_ref[...]
        lane = lax.broadcasted_iota(jnp.int32, gp.shape, 1)
        gates_ref[0, rows, :] = jnp.where(lane < A_HEADS, gp, jax.nn.log_sigmoid(gp))


def _a_in(x, ada_l, g, w_main, w_gates, conv_w, conv_b, b_if):
    nb, s, d = x.shape
    tm = min(PROJ_TM, s)
    halo = SUBLANES
    nqk = 2 * A_QK
    hb = tm // halo
    tok = lambda b, i: (b, i, 0)
    return pl.pallas_call(
        _a_in_kernel,
        grid=(nb, s // tm),
        in_specs=[pl.BlockSpec((1, tm, d), tok),
                  pl.BlockSpec((1, halo, d), lambda b, i: (b, jnp.maximum(i * hb - 1, 0), 0)),
                  pl.BlockSpec((1, 9, d), lambda b, i: (b, 0, 0)),
                  pl.BlockSpec((1, d), lambda b, i: (0, 0)),
                  pl.BlockSpec(w_main.shape, lambda b, i: (0, 0)),
                  pl.BlockSpec(w_gates.shape, lambda b, i: (0, 0)),
                  pl.BlockSpec((A_CONV, nqk), lambda b, i: (0, 0)),
                  pl.BlockSpec((1, nqk), lambda b, i: (0, 0)),
                  pl.BlockSpec((1, LANES), lambda b, i: (0, 0))],
        out_specs=[pl.BlockSpec((1, tm, A_QK), tok),
                   pl.BlockSpec((1, tm, A_QK), tok),
                   pl.BlockSpec((1, tm, A_V), tok),
                   pl.BlockSpec((1, tm, A_V), tok),
                   pl.BlockSpec((1, tm, LANES), tok)],
        out_shape=[jax.ShapeDtypeStruct((nb, s, A_QK), BF16),
                   jax.ShapeDtypeStruct((nb, s, A_QK), BF16),
                   jax.ShapeDtypeStruct((nb, s, A_V), BF16),
                   jax.ShapeDtypeStruct((nb, s, A_V), F32),
                   jax.ShapeDtypeStruct((nb, s, LANES), F32)],
        scratch_shapes=[pltpu.VMEM((halo + tm, nqk), F32)],
        compiler_params=_cparams(("parallel", "arbitrary")),
        name="mlstm_in",
    )(x, x, ada_l, g.reshape(1, d), w_main, w_gates, conv_w, conv_b.reshape(1, nqk), b_if)


def _scan_rows(x, op, fill):
    n = x.shape[0]
    row = lax.broadcasted_iota(jnp.int32, x.shape, 0)
    d = 1
    while d < n:
        x = op(x, jnp.where(row >= d, pltpu.roll(x, d, axis=0), fill))
        d *= 2
    return x


def _a_cell_kernel(q_ref, k_ref, v_ref, o_ref, gates_ref, ng_ref, y_ref,
                   c_scr, n_scr, m_scr, *, L):

    nbk = q_ref.shape[0]

    @pl.when(pl.program_id(0) == 0)
    def _():
        c_scr[...] = jnp.zeros_like(c_scr)
        n_scr[...] = jnp.zeros_like(n_scr)
        m_scr[...] = jnp.zeros_like(m_scr)

    row = lax.broadcasted_iota(jnp.int32, (L, L), 0)
    col = lax.broadcasted_iota(jnp.int32, (L, L), 1)
    causal = col <= row

    def gate_terms(bi, rows):
        gates = gates_ref[bi, rows, :]
        log_f = pltpu.roll(gates, LANES - A_HEADS, axis=1)
        b = _scan_rows(log_f, jnp.add, 0.0)
        g = gates - b
        m_prev = m_scr[bi]
        big_m = jnp.maximum(m_prev, _scan_rows(g, jnp.maximum, NEG_BIG))
        inter = jnp.exp(m_prev - big_m)
        m_t = b + big_m
        m_scr[bi] = m_t[L - 1:L, :]
        return dict(big_m=big_m, inter=inter, floor=jnp.exp(-m_t),
                    w_col=jnp.exp(g - big_m[L - 1:L, :]), decay=inter[L - 1:L, :],
                    g_t=jnp.transpose(g))

    def head(bi, h, rows, t):
        qh = q_ref[bi, rows, h * A_DK:(h + 1) * A_DK]
        kh = k_ref[bi, rows, h * A_DK:(h + 1) * A_DK]
        vh = v_ref[bi, rows, h * A_DV:(h + 1) * A_DV]
        st = bi * A_HEADS + h
        dw = jnp.exp(jnp.where(causal, t["g_t"][h:h + 1, :] - t["big_m"][:, h:h + 1], NEG_BIG))
        s = lax.dot_general(qh, kh, (((1,), (1,)), ((), ())), preferred_element_type=F32) * dw
        inter_h = t["inter"][:, h:h + 1]
        c_h = c_scr[st]
        num = inter_h * jnp.dot(qh, c_h.astype(BF16), preferred_element_type=F32) \
            + jnp.dot(s.astype(BF16), vh, preferred_element_type=F32)
        qn = jnp.sum(qh.astype(F32) * n_scr[st], axis=-1, keepdims=True)
        den = inter_h * qn + jnp.sum(s, axis=-1, keepdims=True)
        hv = num / jnp.maximum(jnp.abs(den), t["floor"][:, h:h + 1])
        hv = hv * lax.rsqrt(jnp.mean(hv * hv, axis=-1, keepdims=True) + EPS)
        og = o_ref[bi, rows, h * A_DV:(h + 1) * A_DV]
        y = jax.nn.sigmoid(og) * (hv * ng_ref[:, h * A_DV:(h + 1) * A_DV])
        y_ref[bi, rows, h * A_DV:(h + 1) * A_DV] = y.astype(BF16)

        kw = kh.astype(F32) * t["w_col"][:, h:h + 1]
        dec = t["decay"][:, h:h + 1]
        c_scr[st] = dec * c_h + jnp.dot(jnp.transpose(kw).astype(BF16), vh,
                                        preferred_element_type=F32)
        n_scr[st] = dec * n_scr[st] + jnp.sum(kw, axis=0, keepdims=True)

    for r in range(q_ref.shape[1] // L):
        rows = slice(r * L, (r + 1) * L)
        terms = [gate_terms(bi, rows) for bi in range(nbk)]
        for h in range(A_HEADS):
            for bi in range(nbk):
                head(bi, h, rows, terms[bi])


def _a_cell(q, k, v, o, gates, norm_g):
    nb, s, _ = q.shape
    chunk = min(CELL_L, s)
    L = min(CELL_L * CELL_NCH, s)
    tok = lambda c: (0, c, 0)
    return pl.pallas_call(
        functools.partial(_a_cell_kernel, L=chunk),
        grid=(s // L,),
        in_specs=[pl.BlockSpec((nb, L, A_QK), tok),
                  pl.BlockSpec((nb, L, A_QK), tok),
                  pl.BlockSpec((nb, L, A_V), tok),
                  pl.BlockSpec((nb, L, A_V), tok),
                  pl.BlockSpec((nb, L, LANES), tok),
                  pl.BlockSpec((1, A_V), lambda c: (0, 0))],
        out_specs=pl.BlockSpec((nb, L, A_V), tok),
        out_shape=jax.ShapeDtypeStruct((nb, s, A_V), BF16),
        scratch_shapes=[pltpu.VMEM((nb * A_HEADS, A_DK, A_DV), F32),
                        pltpu.VMEM((nb * A_HEADS, 1, A_DK), F32),
                        pltpu.VMEM((nb, 1, LANES), F32)],
        compiler_params=_cparams(("arbitrary",)),
        name="mlstm_cell",
    )(q, k, v, o, gates, norm_g.reshape(1, A_V))


def _rope_tab_kernel(pos_ref, invf_ref, cos_ref, sin_ref):
    ang = pos_ref[0].astype(F32) * invf_ref[...]
    lane = lax.broadcasted_iota(jnp.int32, ang.shape, 1)
    half = ROPE_DIM // 2
    sn = jnp.sin(ang)
    cos_ref[0] = jnp.where(lane < ROPE_DIM, jnp.cos(ang), 1.0)
    sin_ref[0] = jnp.where(lane < half, -sn, jnp.where(lane < ROPE_DIM, sn, 0.0))


def _rope_tables(positions):
    nb, s = positions.shape
    half = ROPE_DIM // 2
    inv_freq = ROPE_THETA ** (-jnp.arange(0, ROPE_DIM, 2, dtype=F32) / ROPE_DIM)
    invf = jnp.zeros((1, LANES), F32).at[0, :ROPE_DIM].set(jnp.tile(inv_freq, 2))
    tm = min(1024, s)
    tok = lambda b, i: (b, i, 0)
    return pl.pallas_call(
        _rope_tab_kernel,
        grid=(nb, s // tm),
        in_specs=[pl.BlockSpec((1, tm, 1), tok),
                  pl.BlockSpec((1, LANES), lambda b, i: (0, 0))],
        out_specs=[pl.BlockSpec((1, tm, LANES), tok), pl.BlockSpec((1, tm, LANES), tok)],
        out_shape=[jax.ShapeDtypeStruct((nb, s, LANES), F32)] * 2,
        compiler_params=_cparams(("parallel", "parallel")),
        name="rope_tables",
    )(positions.reshape(nb, s, 1), invf)


def _rope(x, cos_t, sin_t):
    half = ROPE_DIM // 2
    lane = lax.broadcasted_iota(jnp.int32, cos_t.shape, 1)
    outs = []
    for c in range(x.shape[1] // B_DH):
        xg = x[:, c * B_DH:(c + 1) * B_DH]
        swapped = jnp.where(lane < half, pltpu.roll(xg, B_DH - half, axis=1),
                            pltpu.roll(xg, half, axis=1))
        outs.append(xg * cos_t + swapped * sin_t)
    return jnp.concatenate(outs, axis=1)


def _b_in_kernel(x_ref, ada_ref, g_ref, w_ref, cos_ref, sin_ref, q_ref, k_ref, v_ref):
    shift, scale, _ = _ada_rows(ada_ref, 1)
    h = _modulate(x_ref[0], g_ref[...], shift, scale).astype(BF16)
    cos_t = cos_ref[0]
    sin_t = sin_ref[0]
    q = jnp.dot(h, w_ref[:, 0:B_QK], preferred_element_type=F32)
    q_ref[0] = (_rope(q, cos_t, sin_t) * (B_DH ** -0.5 * LOG2E)).astype(BF16)
    k = jnp.dot(h, w_ref[:, B_QK:2 * B_QK], preferred_element_type=F32)
    k_ref[0] = _rope(k, cos_t, sin_t).astype(BF16)
    v_ref[0] = jnp.dot(h, w_ref[:, 2 * B_QK:2 * B_QK + B_V],
                       preferred_element_type=F32).astype(BF16)


def _b_in(x, ada_l, g, w_in, cos_t, sin_t):
    nb, s, d = x.shape
    tm = min(PROJ_TM, s)
    tok = lambda b, i: (b, i, 0)
    return pl.pallas_call(
        _b_in_kernel,
        grid=(nb, s // tm),
        in_specs=[pl.BlockSpec((1, tm, d), tok),
                  pl.BlockSpec((1, 9, d), lambda b, i: (b, 0, 0)),
                  pl.BlockSpec((1, d), lambda b, i: (0, 0)),
                  pl.BlockSpec(w_in.shape, lambda b, i: (0, 0)),
                  pl.BlockSpec((1, tm, LANES), tok),
                  pl.BlockSpec((1, tm, LANES), tok)],
        out_specs=[pl.BlockSpec((1, tm, B_QK), tok),
                   pl.BlockSpec((1, tm, B_QK), tok),
                   pl.BlockSpec((1, tm, B_V), tok)],
        out_shape=[jax.ShapeDtypeStruct((nb, s, B_QK), BF16),
                   jax.ShapeDtypeStruct((nb, s, B_QK), BF16),
                   jax.ShapeDtypeStruct((nb, s, B_V), BF16)],
        compiler_params=_cparams(("parallel", "parallel")),
        name="attn_in",
    )(x, ada_l, g.reshape(1, d), w_in, cos_t, sin_t)


def _b_attn_kernel(q_ref, k_ref, v_ref, lam_ref, ng_ref, y_ref, m_scr, l_scr, acc_scr,
                   sa_scr, sb_scr, *, lam_init, tk):
    qi = pl.program_id(2)
    tq = q_ref.shape[1]
    nchunk = tk // LANES

    m_scr[...] = jnp.full_like(m_scr, NEG_BIG)
    l_scr[...] = jnp.zeros_like(l_scr)
    acc_scr[...] = jnp.zeros_like(acc_scr)

    def scores(j, dst):
        ks = pl.multiple_of(j * tk, tk)
        for c in range(2):
            qc = q_ref[0, :, c * B_DH:(c + 1) * B_DH]
            kc = k_ref[0, pl.ds(ks, tk), c * B_DH:(c + 1) * B_DH]
            dst[c] = lax.dot_general(qc, kc, (((1,), (1,)), ((), ())),
                                     preferred_element_type=F32)

    def update(j, src, masked):
        ks = pl.multiple_of(j * tk, tk)
        v = v_ref[0, pl.ds(ks, tk), :]
        if masked:
            row = lax.broadcasted_iota(jnp.int32, (tq, tk), 0)
            col = lax.broadcasted_iota(jnp.int32, (tq, tk), 1)
            keep = col <= row
        for c in range(2):
            s = src[c]
            if masked:
                s = jnp.where(keep, s, NEG_BIG)
            m_old = m_scr[c]
            m_new = jnp.maximum(m_old, jnp.max(s, axis=-1, keepdims=True))
            alpha = jnp.exp2(m_old - m_new)
            chunks = [jnp.exp2(s[:, n * LANES:(n + 1) * LANES] - m_new) for n in range(nchunk)]
            part = chunks[0]
            for n in range(1, nchunk):
                part = part + chunks[n]
            l_scr[c] = alpha * l_scr[c] + part
            pr = jnp.concatenate(chunks, axis=1).astype(BF16)
            alpha_v = jnp.concatenate([alpha] * (B_DV // LANES), axis=1)
            acc_scr[c] = alpha_v * acc_scr[c] + jnp.dot(pr, v, preferred_element_type=F32)
            m_scr[c] = m_new

    buf_a = sa_scr
    buf_b = sb_scr
    scores(0, buf_a)

    def body(i, carry):
        j = 2 * i
        scores(j + 1, buf_b)
        update(j, buf_a, False)
        scores(j + 2, buf_a)
        update(j + 1, buf_b, False)
        return carry

    lax.fori_loop(0, qi // 2, body, 0)

    @pl.when(qi % 2 == 1)
    def _():
        scores(qi, buf_b)
        update(qi - 1, buf_a, False)
        update(qi, buf_b, True)

    @pl.when(qi % 2 == 0)
    def _():
        update(qi, buf_a, True)

    lf = lam_ref[...]
    lam_full = (jnp.exp(jnp.sum(lf[0:1] * lf[1:2], axis=-1, keepdims=True))
                - jnp.exp(jnp.sum(lf[2:3] * lf[3:4], axis=-1, keepdims=True)) + lam_init)
    l0 = jnp.sum(l_scr[0], axis=-1, keepdims=True)
    l1 = jnp.sum(l_scr[1], axis=-1, keepdims=True)
    o = acc_scr[0] / l0 - lam_full * (acc_scr[1] / l1)
    o = o * lax.rsqrt(jnp.mean(o * o, axis=-1, keepdims=True) + EPS)
    y_ref[0] = ((o * ng_ref[...]) * (1.0 - lam_init)).astype(BF16)


def _b_attn(q, k, v, lam, norm_g, lam_init):
    nb, s, _ = q.shape
    tq = min(ATT_TQ, s)
    tk = tq
    return pl.pallas_call(
        functools.partial(_b_attn_kernel, lam_init=lam_init, tk=tk),
        grid=(nb, B_HEADS, s // tq),
        in_specs=[pl.BlockSpec((1, tq, 2 * B_DH), lambda b, h, i: (b, i, h)),
                  pl.BlockSpec((1, s, 2 * B_DH), lambda b, h, i: (b, 0, h)),
                  pl.BlockSpec((1, s, B_DV), lambda b, h, i: (b, 0, h)),
                  pl.BlockSpec((4, B_DH), lambda b, h, i: (0, 0)),
                  pl.BlockSpec((1, B_DV), lambda b, h, i: (0, 0))],
        out_specs=pl.BlockSpec((1, tq, B_DV), lambda b, h, i: (b, i, h)),
        out_shape=jax.ShapeDtypeStruct((nb, s, B_V), BF16),
        scratch_shapes=[pltpu.VMEM((2, tq, LANES), F32),
                        pltpu.VMEM((2, tq, LANES), F32),
                        pltpu.VMEM((2, tq, B_DV), F32),
                        pltpu.VMEM((2, tq, tk), F32),
                        pltpu.VMEM((2, tq, tk), F32)],
        compiler_params=_cparams(("parallel", "parallel", "arbitrary")),
        name="diff_attn",
    )(q, k, v, lam, norm_g.reshape(1, B_DV))


def kernel(x, c, positions, ada_w, ada_b, norm_g, ffn_w13, ffn_w2, a_w_in, a_conv_w, a_conv_b,
           a_b_if, a_norm_g, a_w_out, b_w_in, b_lam, b_norm_g, b_w_out, final_g):
    depth = ada_w.shape[0]
    d = x.shape[-1]
    ada = _ada_all(c, ada_w, ada_b)
    cos_t, sin_t = _rope_tables(positions)

    nqk = 2 * A_QK
    n_main = nqk + 2 * A_V
    w13_bf = ffn_w13.astype(BF16)
    w2_bf = ffn_w2.astype(BF16)
    for i in range(depth):
        ada_l = ada[i]
        x = _ffn(x, ada_l, norm_g[i, 0], w13_bf, w2_bf, (i, 0), final_g, sub=0, final_norm=False)
        j = i // N_MIXERS
        if i % N_MIXERS == 0:
            w_in = a_w_in[j]
            w_main = w_in[:, :n_main].astype(BF16)
            w_gates = jnp.pad(w_in[:, n_main:], ((0, 0), (0, LANES - 2 * A_HEADS))).astype(BF16)
            b_if = jnp.pad(a_b_if[j], (0, LANES - 2 * A_HEADS)).reshape(1, LANES)
            q, k, v, o, gates = _a_in(x, ada_l, norm_g[i, 1], w_main, w_gates,
                                      a_conv_w[j], a_conv_b[j], b_if)
            y = _a_cell(q, k, v, o, gates, a_norm_g[j])
            w_out = a_w_out[j].astype(BF16)
        else:
            lam_init = 0.8 - 0.6 * math.exp(-0.3 * i)
            q, k, v = _b_in(x, ada_l, norm_g[i, 1], b_w_in[j].astype(BF16), cos_t, sin_t)
            y = _b_attn(q, k, v, b_lam[j], b_norm_g[j], lam_init)
            w_out = b_w_out[j].astype(BF16)
        x = _ffn(x, ada_l, norm_g[i, 2], w13_bf, w2_bf, (i, 1), final_g, sub=2,
                 final_norm=(i == depth - 1), mix=(y, w_out))
    return x
```
